```python
import math
import jax, jax.numpy as jnp
from jax import lax
import numpy as np

D_MODEL = 1024
BATCH = 4
SEQ = 8192
DEPTH = 2

GRID_W = 64
CTX_LEN = 256
Q_BLOCK = 128
ROPE_BASE = 10000.0
LN_EPS = 1e-6
RMS_EPS = 1e-6
SUBLN_EPS = 1e-5
DA_HEADS = D_MODEL // 256
DA_DIM = 64
DA_VDIM = 2 * DA_DIM
MLA_HEADS = D_MODEL // 256
MLA_Q_RANK = D_MODEL // 4
MLA_KV_RANK = D_MODEL // 8
MLA_NOPE = 64
MLA_ROPE = 32
MLA_VDIM = 64
NA_HEADS = D_MODEL // 256
NA_DIM = 64
NA_KH = 8
NA_KW = 16
D_FF = 4 * D_MODEL
MIX_WIDTH = DA_HEADS * DA_VDIM + MLA_HEADS * MLA_VDIM + NA_HEADS * NA_DIM
IN_SIZES = (DA_HEADS * 2 * DA_DIM, DA_HEADS * 2 * DA_DIM, DA_HEADS * DA_VDIM,
            MLA_Q_RANK, MLA_KV_RANK, MLA_ROPE,
            NA_HEADS * NA_DIM, NA_HEADS * NA_DIM, NA_HEADS * NA_DIM)
IN_WIDTH = sum(IN_SIZES)
DN_ALPHA = (2.0 * DEPTH) ** 0.25
DN_BETA = (8.0 * DEPTH) ** -0.25

kernel_name = 'hybrid_parallel_head_dit_block'


def layer_norm(x, g, b):
    xf = x.astype(jnp.float32)
    mu = jnp.mean(xf, axis=-1, keepdims=True)
    var = jnp.mean(jnp.square(xf - mu), axis=-1, keepdims=True)
    y = (xf - mu) * lax.rsqrt(var + LN_EPS) * g.astype(jnp.float32) + b.astype(jnp.float32)
    return y.astype(x.dtype)


def rms_norm(x, g, eps):
    xf = x.astype(jnp.float32)
    y = xf * lax.rsqrt(jnp.mean(jnp.square(xf), axis=-1, keepdims=True) + eps)
    return (y * g.astype(jnp.float32)).astype(x.dtype)


def softmax_f32(s):
    return jax.nn.softmax(s.astype(jnp.float32), axis=-1)


def modulate(x, shift, scale):
    return x * (1 + scale) + shift


def rope_1d(x, pos):
    n = x.shape[-1]
    half = n // 2
    inv_freq = ROPE_BASE ** (-2.0 * jnp.arange(half, dtype=jnp.float32) / n)
    ang = pos.astype(jnp.float32)[:, None] * inv_freq[None, :]
    shape = (ang.shape[0],) + (1,) * (x.ndim - 3) + (half,)
    cos = jnp.cos(ang).reshape(shape).astype(x.dtype)
    sin = jnp.sin(ang).reshape(shape).astype(x.dtype)
    x1, x2 = x[..., :half], x[..., half:]
    return jnp.concatenate([x1 * cos - x2 * sin, x2 * cos + x1 * sin], axis=-1)


def axial_rope(x, row, col):
    n = x.shape[-1] // 2
    return jnp.concatenate([rope_1d(x[..., :n], row), rope_1d(x[..., n:], col)], axis=-1)


def split_cols(p):
    offs = [int(o) for o in np.cumsum(IN_SIZES)[:-1]]
    return jnp.split(p, offs, axis=-1)


def sweep_query_blocks(fn, *qs):
    b, length = qs[0].shape[:2]
    nb = length // Q_BLOCK
    blocks = tuple(jnp.moveaxis(q.reshape((b, nb, Q_BLOCK) + q.shape[2:]), 1, 0) for q in qs)
    out = lax.map(lambda qb: fn(*qb), blocks)
    return jnp.moveaxis(out, 0, 1).reshape((b, length) + out.shape[3:])


def da_heads(q, k, v):
    b, n = q.shape[:2]
    q = q.reshape(b, n, DA_HEADS, 2, DA_DIM)
    k = k.reshape(b, n, DA_HEADS, 2, DA_DIM)
    return (q[..., 0, :], q[..., 1, :], k[..., 0, :], k[..., 1, :],
            v.reshape(b, n, DA_HEADS, DA_VDIM))


def diff_attention(q1, q2, k1, k2, v, lam, subln_g, lam_init):
    scale = DA_DIM ** -0.5
    a1 = softmax_f32(jnp.einsum('bqhd,bkhd->bhqk', q1, k1) * scale)
    a2 = softmax_f32(jnp.einsum('bqhd,bkhd->bhqk', q2, k2) * scale)
    p = (a1 - lam * a2).astype(v.dtype)
    o = jnp.einsum('bhqk,bkhe->bqhe', p, v)
    return rms_norm(o, subln_g, SUBLN_EPS) * (1.0 - lam_init)


def mla_heads(cq, ckv, kr, q_norm_g, w_uq, kv_norm_g, w_ukv):
    b, n = cq.shape[:2]
    q = (rms_norm(cq, q_norm_g, RMS_EPS) @ w_uq).reshape(b, n, MLA_HEADS, MLA_NOPE + MLA_ROPE)
    kv = (rms_norm(ckv, kv_norm_g, RMS_EPS) @ w_ukv).reshape(b, n, MLA_HEADS, MLA_NOPE + MLA_VDIM)
    return q[..., :MLA_NOPE], q[..., MLA_NOPE:], kv[..., :MLA_NOPE], kr, kv[..., MLA_NOPE:]


def mla_attention(q_nope, q_pe, k_nope, k_pe, v):
    scale = (MLA_NOPE + MLA_ROPE) ** -0.5
    s = (jnp.einsum('bqhd,bkhd->bhqk', q_nope, k_nope)
         + jnp.einsum('bqhr,bkr->bhqk', q_pe, k_pe))
    p = softmax_f32(s * scale).astype(v.dtype)
    return jnp.einsum('bhqk,bkhd->bqhd', p, v)


def dense_attention(q, k, v):
    s = jnp.einsum('bqhd,bkhd->bhqk', q, k) * (q.shape[-1] ** -0.5)
    p = softmax_f32(s).astype(v.dtype)
    return jnp.einsum('bhqk,bkhd->bqhd', p, v)


def neighbourhood_attention(q, k, v, kc, vc, rpb):
    b, length, nh, d = q.shape
    rows = length // GRID_W
    kh = min(NA_KH, rows)
    qg = q.reshape(b, rows, GRID_W, nh, d)
    kg = k.reshape(b, rows, GRID_W, nh, d)
    vg = v.reshape(b, rows, GRID_W, nh, d)
    cols = np.arange(GRID_W)
    col_start = np.clip(cols - NA_KW // 2, 0, GRID_W - NA_KW)
    col_idx = col_start[:, None] + np.arange(NA_KW)[None, :]
    dx = col_idx - cols[:, None] + (NA_KW - 1)
    bias_cols = rpb[:, :, dx]
    scale = d ** -0.5
    n_win = kh * NA_KW

    def row_block(r):
        rs = jnp.clip(r - kh // 2, 0, rows - kh)
        k_win = lax.dynamic_slice_in_dim(kg, rs, kh, axis=1)[:, :, col_idx]
        v_win = lax.dynamic_slice_in_dim(vg, rs, kh, axis=1)[:, :, col_idx]
        q_r = lax.dynamic_index_in_dim(qg, r, axis=1, keepdims=False)
        dy = rs + jnp.arange(kh) - r + (NA_KH - 1)
        bias = jnp.transpose(jnp.take(bias_cols, dy, axis=1), (0, 2, 1, 3))
        s_win = (jnp.einsum('bqhd,biqjhd->bhqij', q_r, k_win) * scale).astype(jnp.float32) + bias[None].astype(jnp.float32)
        s_ctx = jnp.einsum('bqhd,bchd->bhqc', q_r, kc) * scale
        s = jnp.concatenate([s_win.reshape(b, nh, GRID_W, n_win), s_ctx.astype(jnp.float32)], axis=-1)
        p = softmax_f32(s).astype(v.dtype)
        p_win = p[..., :n_win].reshape(b, nh, GRID_W, kh, NA_KW)
        return (jnp.einsum('bhqij,biqjhd->bqhd', p_win, v_win)
                + jnp.einsum('bhqc,bchd->bqhd', p[..., n_win:], vc))

    out = lax.map(row_block, jnp.arange(rows))
    return jnp.moveaxis(out, 0, 1).reshape(b, length, nh * d)


def token_mixers(h, hc, row, col, lam_init, need_ctx, w_in, lam_q1, lam_k1, lam_q2, lam_k2,
                 subln_g, q_norm_g, w_uq, kv_norm_g, w_ukv, rpb):
    b, length, _ = h.shape
    lc = hc.shape[1]
    aq, ak, av, cq, ckv, kr, nq, nk, nv = split_cols(h @ w_in)
    aqc, akc, avc, cqc, ckvc, krc, nqc, nkc, nvc = split_cols(hc @ w_in)

    f32 = jnp.float32
    lam = (jnp.exp(jnp.sum(lam_q1.astype(f32) * lam_k1.astype(f32)))
           - jnp.exp(jnp.sum(lam_q2.astype(f32) * lam_k2.astype(f32))) + lam_init)
    q1, q2, k1, k2, va = da_heads(aq, ak, av)
    q1, q2, k1, k2 = (axial_rope(t, row, col) for t in (q1, q2, k1, k2))
    q1c, q2c, k1c, k2c, vac = da_heads(aqc, akc, avc)
    k1_all = jnp.concatenate([k1c, k1], axis=1)
    k2_all = jnp.concatenate([k2c, k2], axis=1)
    va_all = jnp.concatenate([vac, va], axis=1)
    y_a = sweep_query_blocks(
        lambda qa, qb: diff_attention(qa, qb, k1_all, k2_all, va_all, lam, subln_g, lam_init),
        q1, q2).reshape(b, length, -1)

    qn, qp, kn, kp, vm = mla_heads(cq, ckv, kr, q_norm_g, w_uq, kv_norm_g, w_ukv)
    qp = axial_rope(qp, row, col)
    kp = axial_rope(kp, row, col)
    qnc, qpc, knc, kpc, vmc = mla_heads(cqc, ckvc, krc, q_norm_g, w_uq, kv_norm_g, w_ukv)
    kn_all = jnp.concatenate([knc, kn], axis=1)
    kp_all = jnp.concatenate([kpc, kp], axis=1)
    vm_all = jnp.concatenate([vmc, vm], axis=1)
    y_b = sweep_query_blocks(
        lambda a, p_: mla_attention(a, p_, kn_all, kp_all, vm_all), qn, qp).reshape(b, length, -1)

    nq, nk, nv = (t.reshape(b, length, NA_HEADS, NA_DIM) for t in (nq, nk, nv))
    nqc, nkc, nvc = (t.reshape(b, lc, NA_HEADS, NA_DIM) for t in (nqc, nkc, nvc))
    y_c = neighbourhood_attention(nq, nk, nv, nkc, nvc, rpb)

    y = jnp.concatenate([y_a, y_b, y_c], axis=-1)
    if not need_ctx:
        return y, None
    y_ctx = jnp.concatenate([
        diff_attention(q1c, q2c, k1c, k2c, vac, lam, subln_g, lam_init).reshape(b, lc, -1),
        mla_attention(qnc, qpc, knc, kpc, vmc).reshape(b, lc, -1),
        dense_attention(nqc, nkc, nvc).reshape(b, lc, -1)], axis=-1)
    return y, y_ctx


def sq_relu_mlp(h, w1, w2):
    return jnp.square(jax.nn.relu(h @ w1)) @ w2


def setup_inputs(seed: int = 0) -> dict:
    key = jax.random.key(seed)
    ks = jax.random.split(key, 24)

    def nrm(k, shape, s):
        return jax.random.normal(k, shape, jnp.float32) * s

    nl = DEPTH
    return {
        'x': nrm(ks[0], (BATCH, SEQ, D_MODEL), 1.0),
        'c': nrm(ks[1], (BATCH, D_MODEL), 1.0),
        'ctx': nrm(ks[2], (BATCH, CTX_LEN, D_MODEL), 1.0),
        'c_ctx': nrm(ks[3], (D_MODEL,), 1.0),
        'w_mod': nrm(ks[4], (nl, D_MODEL, 6 * D_MODEL), 0.5 * D_MODEL ** -0.5),
        'b_mod': nrm(ks[5], (nl, 6 * D_MODEL), 0.01),
        'w_in': nrm(ks[6], (nl, D_MODEL, IN_WIDTH), D_MODEL ** -0.5),
        'da_lam_q1': nrm(ks[7], (nl, DA_DIM), 0.1),
        'da_lam_k1': nrm(ks[8], (nl, DA_DIM), 0.1),
        'da_lam_q2': nrm(ks[9], (nl, DA_DIM), 0.1),
        'da_lam_k2': nrm(ks[10], (nl, DA_DIM), 0.1),
        'da_subln_g': 1.0 + nrm(ks[11], (nl, DA_VDIM), 0.01),
        'mla_q_norm_g': 1.0 + nrm(ks[12], (nl, MLA_Q_RANK), 0.01),
        'mla_w_uq': nrm(ks[13], (nl, MLA_Q_RANK, MLA_HEADS * (MLA_NOPE + MLA_ROPE)), MLA_Q_RANK ** -0.5),
        'mla_kv_norm_g': 1.0 + nrm(ks[14], (nl, MLA_KV_RANK), 0.01),
        'mla_w_ukv': nrm(ks[15], (nl, MLA_KV_RANK, MLA_HEADS * (MLA_NOPE + MLA_VDIM)), MLA_KV_RANK ** -0.5),
        'na_rpb': nrm(ks[16], (nl, NA_HEADS, 2 * NA_KH - 1, 2 * NA_KW - 1), 0.05),
        'w_out': nrm(ks[17], (nl, MIX_WIDTH, D_MODEL), DN_BETA * MIX_WIDTH ** -0.5),
        'ln1_g': 1.0 + nrm(ks[18], (nl, D_MODEL), 0.01),
        'ln1_b': nrm(ks[19], (nl, D_MODEL), 0.01),
        'w_ff1': nrm(ks[20], (nl, D_MODEL, D_FF), D_MODEL ** -0.5),
        'w_ff2': nrm(ks[21], (nl, D_FF, D_MODEL), DN_BETA * D_FF ** -0.5),
        'ln2_g': 1.0 + nrm(ks[22], (nl, D_MODEL), 0.01),
        'ln2_b': nrm(ks[23], (nl, D_MODEL), 0.01),
    }


def reference(x, c, ctx, c_ctx, w_mod, b_mod, w_in, da_lam_q1, da_lam_k1, da_lam_q2, da_lam_k2,
              da_subln_g, mla_q_norm_g, mla_w_uq, mla_kv_norm_g, mla_w_ukv, na_rpb, w_out,
              ln1_g, ln1_b, w_ff1, w_ff2, ln2_g, ln2_b):
    length = x.shape[1]
    t = jnp.arange(length, dtype=jnp.int32)
    row = t // GRID_W
    col = t % GRID_W
    s_c = jax.nn.silu(c)
    s_cc = jax.nn.silu(c_ctx)
    xc = ctx
    for l in range(DEPTH):
        last = l == DEPTH - 1
        lam_init = 0.8 - 0.6 * math.exp(-0.3 * l)
        mod = jnp.split(s_c @ w_mod[l] + b_mod[l], 6, axis=-1)
        shift1, scale1, gate1, shift2, scale2, gate2 = (m[:, None, :] for m in mod)
        mod_c = jnp.split(s_cc @ w_mod[l] + b_mod[l], 6, axis=-1)
        y, yc = token_mixers(modulate(x, shift1, scale1), modulate(xc, mod_c[0], mod_c[1]),
                             row, col, lam_init, not last, w_in[l],
                             da_lam_q1[l], da_lam_k1[l], da_lam_q2[l], da_lam_k2[l], da_subln_g[l],
                             mla_q_norm_g[l], mla_w_uq[l], mla_kv_norm_g[l], mla_w_ukv[l], na_rpb[l])
        x = layer_norm(DN_ALPHA * x + gate1 * (y @ w_out[l]), ln1_g[l], ln1_b[l])
        x = layer_norm(DN_ALPHA * x + gate2 * sq_relu_mlp(modulate(x, shift2, scale2), w_ff1[l], w_ff2[l]),
                       ln2_g[l], ln2_b[l])
        if not last:
            xc = layer_norm(DN_ALPHA * xc + mod_c[2] * (yc @ w_out[l]), ln1_g[l], ln1_b[l])
            xc = layer_norm(DN_ALPHA * xc + mod_c[5] * sq_relu_mlp(modulate(xc, mod_c[3], mod_c[4]), w_ff1[l], w_ff2[l]),
                            ln2_g[l], ln2_b[l])
    return x
```

```python
import functools
import math

import numpy as np
import jax
import jax.numpy as jnp
from jax import lax
from jax.experimental import pallas as pl
from jax.experimental.pallas import tpu as pltpu

F32 = jnp.float32
BF16 = jnp.bfloat16

GRID_W = 64
ROPE_BASE = 10000.0
LN_EPS = 1e-6
RMS_EPS = 1e-6
SUBLN_EPS = 1e-5
N_HEADS = 4
DA_DIM = 64
DA_VDIM = 128
MLA_Q_RANK = 256
MLA_KV_RANK = 128
MLA_NOPE = 64
MLA_ROPE = 32
MLA_VDIM = 64
NA_DIM = 64
NA_KH = 8
NA_KW = 16
HEAD_PAD = 128

LOG2E = math.log2(math.e)
NEG_BIG = -1e30

TOK_TILE = 256
NA_Q_ROWS = 4
NA_WIN_ROWS = NA_Q_ROWS + NA_KH
VMEM_LIMIT = 56 * 1024 * 1024


def _cparams(sem):
    return pltpu.CompilerParams(dimension_semantics=sem, vmem_limit_bytes=VMEM_LIMIT)


def _dot(a, b):
    return jnp.dot(a, b, preferred_element_type=F32)


def _mod_kernel(c_ref, w_ref, b_ref, o_ref):
    c = c_ref[...]
    s = c / (1.0 + jnp.exp(-c))
    o_ref[...] = jnp.dot(s, w_ref[...], preferred_element_type=F32,
                         precision=lax.Precision.HIGHEST) + b_ref[...]


def _modulation(cs, w_mod, b_mod):
    depth, d, n = w_mod.shape
    nb = 1024
    return pl.pallas_call(
        _mod_kernel,
        grid=(depth, n // nb),
        in_specs=[pl.BlockSpec((8, d), lambda l, j: (0, 0)),
                  pl.BlockSpec((None, d, nb), lambda l, j: (l, 0, j)),
                  pl.BlockSpec((None, 1, nb), lambda l, j: (l, 0, j))],
        out_specs=pl.BlockSpec((None, 8, nb), lambda l, j: (l, 0, j)),
        out_shape=jax.ShapeDtypeStruct((depth, 8, n), F32),
        compiler_params=_cparams(("arbitrary", "arbitrary")),
        name="modulation",
    )(cs, w_mod, b_mod.reshape(depth, 1, n))


def _rms(x, axis, eps):
    return x * lax.rsqrt(jnp.mean(jnp.square(x), axis=axis, keepdims=True) + eps)


def _proj_kernel(x_ref, mod_ref, wtok_ref, wt_ref,
                 ca_ref, sa_ref, cat_ref, sat_ref, cm_ref, sm_ref, cmt_ref, smt_ref,
                 gq_ref, wuq_ref, gkv_ref, gkvt_ref, wuv_ref, wukt_ref,
                 qa_ref, kat_ref, va_ref, qm_ref, kmt_ref, vm_ref, nq_ref, nkt_ref, nv_ref):
    x = x_ref[...]
    h = (x * (1.0 + mod_ref[1:2, :]) + mod_ref[0:1, :]).astype(BF16)
    p = _dot(h, wtok_ref[...])
    pt = lax.dot_general(wt_ref[...], h, (((1,), (1,)), ((), ())),
                         preferred_element_type=F32)
    tm = x.shape[0]
    lane = lax.broadcasted_iota(jnp.int32, (tm, HEAD_PAD), 1)

    ca, sa = ca_ref[...], sa_ref[...]
    for g in range(N_HEADS):
        xq = p[:, g * 128:(g + 1) * 128]
        partner = jnp.where((lane & 16) == 0, pltpu.roll(xq, 128 - 16, 1), pltpu.roll(xq, 16, 1))
        qa_ref[:, g * 128:(g + 1) * 128] = (xq * ca + partner * sa).astype(BF16)
    va_ref[...] = p[:, 512:1024].astype(BF16)
    cat, sat = cat_ref[...], sat_ref[...]
    for g in range(2 * N_HEADS):
        xk = pt[g * 64:(g + 1) * 64, :]
        partner = jnp.concatenate([xk[16:32], xk[0:16], xk[48:64], xk[32:48]], axis=0)
        kat_ref[g * 64:(g + 1) * 64, :] = (xk * cat + partner * sat).astype(BF16)

    cqn = (_rms(p[:, 1024:1280], -1, RMS_EPS) * gq_ref[...]).astype(BF16)
    qm = _dot(cqn, wuq_ref[...])
    cm, sm = cm_ref[...], sm_ref[...]
    for g in range(N_HEADS):
        xq = qm[:, g * 128:(g + 1) * 128]
        partner = jnp.where((lane & 8) == 0, pltpu.roll(xq, 128 - 8, 1), pltpu.roll(xq, 8, 1))
        qm_ref[:, g * 128:(g + 1) * 128] = (xq * cm + partner * sm).astype(BF16)
    ckvn = (_rms(p[:, 1280:1408], -1, RMS_EPS) * gkv_ref[...]).astype(BF16)
    vm_ref[...] = _dot(ckvn, wuv_ref[...]).astype(BF16)
    ckvnt = (_rms(pt[512:640, :], 0, RMS_EPS) * gkvt_ref[...]).astype(BF16)
    knt = _dot(wukt_ref[...], ckvnt)
    xr = pt[640:672, :]
    partner = jnp.concatenate([xr[8:16], xr[0:8], xr[24:32], xr[16:24]], axis=0)
    krt = (xr * cmt_ref[...] + partner * smt_ref[...]).astype(BF16)
    pad = jnp.zeros((HEAD_PAD - MLA_NOPE - MLA_ROPE, tm), BF16)
    for g in range(N_HEADS):
        kmt_ref[g * 128:g * 128 + 64, :] = knt[g * 64:(g + 1) * 64, :].astype(BF16)
        kmt_ref[g * 128 + 64:g * 128 + 96, :] = krt
        kmt_ref[g * 128 + 96:(g + 1) * 128, :] = pad

    nq_ref[...] = (p[:, 1408:1664] * (NA_DIM ** -0.5 * LOG2E)).astype(BF16)
    nv_ref[...] = p[:, 1664:1920].astype(BF16)
    nkt_ref[...] = pt[672:928, :].astype(BF16)


def _projection(xall, mod_l, pw, tabs, n_ctx_row):
    b, t, d = xall.shape
    tm = TOK_TILE
    nt = t // tm

    def tok(width):
        return pl.BlockSpec((None, tm, width), lambda j, i: (i, j, 0))

    def trn(height):
        return pl.BlockSpec((None, height, tm), lambda j, i: (i, 0, j))

    def const(arr):
        return pl.BlockSpec(arr.shape, lambda j, i: (0,) * arr.ndim)

    tab_tok = pl.BlockSpec((tm, HEAD_PAD), lambda j, i: (j, 0))
    consts = [pw["gq"], pw["wuq"], pw["gkv"], pw["gkvt"], pw["wuv"], pw["wukt"]]
    in_specs = [tok(d),
                pl.BlockSpec((None, 6, d), lambda j, i: (jnp.where(j == nt - 1, n_ctx_row, i), 0, 0)),
                const(pw["wtok"]), const(pw["wt"]),
                tab_tok, tab_tok,
                pl.BlockSpec((64, tm), lambda j, i: (0, j)), pl.BlockSpec((64, tm), lambda j, i: (0, j)),
                tab_tok, tab_tok,
                pl.BlockSpec((32, tm), lambda j, i: (0, j)), pl.BlockSpec((32, tm), lambda j, i: (0, j)),
                ] + [const(a) for a in consts]
    widths = dict(qa=512, va=512, qm=512, vm=256, nq=256, nv=256)
    out_specs = [tok(512), trn(512), tok(512), tok(512), trn(512), tok(256), tok(256), trn(256), tok(256)]
    out_shape = [jax.ShapeDtypeStruct((b, t, 512), BF16), jax.ShapeDtypeStruct((b, 512, t), BF16),
                 jax.ShapeDtypeStruct((b, t, 512), BF16), jax.ShapeDtypeStruct((b, t, 512), BF16),
                 jax.ShapeDtypeStruct((b, 512, t), BF16), jax.ShapeDtypeStruct((b, t, 256), BF16),
                 jax.ShapeDtypeStruct((b, t, 256), BF16), jax.ShapeDtypeStruct((b, 256, t), BF16),
                 jax.ShapeDtypeStruct((b, t, 256), BF16)]
    del widths
    return pl.pallas_call(
        _proj_kernel,
        grid=(nt, b),
        in_specs=in_specs,
        out_specs=out_specs,
        out_shape=out_shape,
        compiler_params=_cparams(("parallel", "parallel")),
        name="projection",
    )(xall, mod_l, pw["wtok"], pw["wt"],
      tabs["ca"], tabs["sa"], tabs["cat"], tabs["sat"], tabs["cm"], tabs["sm"], tabs["cmt"], tabs["smt"],
      *consts)


def _attn_kernel(*refs, mode, n_k, tk, lam_init, aliased_out):
    o_ref, m_ref, l_ref, acc_ref = refs[-4:]
    refs = refs[:-5] if aliased_out else refs[:-4]
    if mode == "da":
        q_ref, kt_ref, v_ref, lam_ref, g_ref = refs
    else:
        q_ref, kt_ref, v_ref = refs
    tq = q_ref.shape[0]
    if mode == "da":
        q = q_ref[...]
        lane = lax.broadcasted_iota(jnp.int32, q.shape, 1)
        zero = jnp.zeros_like(q)
        qs = (jnp.where(lane < DA_DIM, q, zero), jnp.where(lane >= DA_DIM, q, zero))
        rows = ((0, 128), (0, 128))
    else:
        qs = (q_ref[:, 0:128], q_ref[:, 128:256])
        rows = ((0, 128), (128, 256))

    m_ref[...] = jnp.full(m_ref.shape, NEG_BIG, F32)
    l_ref[...] = jnp.zeros(l_ref.shape, F32)
    acc_ref[...] = jnp.zeros(acc_ref.shape, F32)

    def chunk(start, size):
        v_c = v_ref[pl.ds(start, size), :]
        for i in range(2):
            kt_c = kt_ref[rows[i][0]:rows[i][1], pl.ds(start, size)]
            s = _dot(qs[i], kt_c)
            m_prev = m_ref[i]
            m_new = jnp.maximum(m_prev, jnp.max(s, axis=-1, keepdims=True))
            p = jnp.exp2(s - m_new)
            alpha = jnp.exp2(m_prev - m_new)
            l_ref[i] = alpha * l_ref[i] + jnp.sum(p, axis=-1, keepdims=True)
            acc_ref[i] = alpha * acc_ref[i] + _dot(p.astype(BF16), v_c)
            m_ref[i] = m_new

    def body(c, carry):
        chunk(pl.multiple_of(c * tk, tk), tk)
        return carry

    n_main = n_k // tk
    lax.fori_loop(0, n_main, body, 0)
    if n_k % tk:
        chunk(n_main * tk, n_k % tk)

    o0 = acc_ref[0] / l_ref[0]
    o1 = acc_ref[1] / l_ref[1]
    if mode == "da":
        lv = lam_ref[...]
        lam = (jnp.exp(jnp.sum(lv[0:1] * lv[1:2], axis=-1, keepdims=True))
               - jnp.exp(jnp.sum(lv[2:3] * lv[3:4], axis=-1, keepdims=True)) + lam_init)
        o = o0 - lam * o1
        o = _rms(o, -1, SUBLN_EPS) * g_ref[...] * (1.0 - lam_init)
    else:
        lane = lax.broadcasted_iota(jnp.int32, (tq, 128), 1)
        o = jnp.where(lane < MLA_VDIM, o0, o1)
    o_ref[...] = o.astype(o_ref.dtype)


def _dense_attention(mode, q, kt, v, *, tq, tk, q_blk0, k_blk0, n_k, out, extra=(), lam_init=0.0):
    b = q.shape[0]
    groups = N_HEADS if mode == "da" else N_HEADS // 2
    qw = 128 if mode == "da" else 256
    aliased = not isinstance(out, int)
    if aliased:
        n_q, out_shape = tq, jax.ShapeDtypeStruct(out.shape, out.dtype)
    else:
        n_q, out_shape = out, jax.ShapeDtypeStruct((b, q.shape[1], 128 * groups), BF16)
    in_specs = [pl.BlockSpec((None, tq, qw), lambda i, g, j: (i, q_blk0 + j, g)),
                pl.BlockSpec((None, qw, n_k), lambda i, g, j: (i, g, k_blk0)),
                pl.BlockSpec((None, n_k, 128), lambda i, g, j: (i, k_blk0, g))]
    args = [q, kt, v]
    for a in extra:
        in_specs.append(pl.BlockSpec(a.shape, lambda i, g, j: (0, 0)))
        args.append(a)
    alias = {}
    if aliased:
        in_specs.append(pl.BlockSpec(memory_space=pl.ANY))
        alias = {len(args): 0}
        args.append(out)
    return pl.pallas_call(
        functools.partial(_attn_kernel, mode=mode, n_k=n_k, tk=tk, lam_init=lam_init, aliased_out=aliased),
        grid=(b, groups, n_q // tq),
        in_specs=in_specs,
        out_specs=pl.BlockSpec((None, tq, 128), lambda i, g, j: (i, q_blk0 + j, g)),
        out_shape=out_shape,
        scratch_shapes=[pltpu.VMEM((2, tq, 1), F32), pltpu.VMEM((2, tq, 1), F32),
                        pltpu.VMEM((2, tq, 128), F32)],
        input_output_aliases=alias,
        compiler_params=_cparams(("parallel", "parallel", "parallel")),
        name="attn_" + mode,
    )(*args)


def _na_kernel(*refs, rows, ctx_off, n_ctx, use_window):
    if use_window:
        q_ref, kt_ref, v_ref, bias_ref, o_ref = refs
    else:
        q_ref, kt_ref, v_ref, o_ref = refs
    q = q_ref[...]
    lane = lax.broadcasted_iota(jnp.int32, q.shape, 1)
    zero = jnp.zeros_like(q)
    kt_c = kt_ref[:, ctx_off:ctx_off + n_ctx]
    v_c = v_ref[ctx_off:ctx_off + n_ctx, :]
    if use_window:
        i = pl.program_id(1)
        kr0 = jnp.clip(i * NA_Q_ROWS - NA_KH // 2, 0, rows - NA_WIN_ROWS)
        start = pl.multiple_of(kr0 * GRID_W, 128)
        kt_w = kt_ref[:, pl.ds(start, NA_WIN_ROWS * GRID_W)]
        v_w = v_ref[pl.ds(start, NA_WIN_ROWS * GRID_W), :]
    out = jnp.zeros(q.shape, F32)
    for h in range(N_HEADS):
        hm = (lane >= h * NA_DIM) & (lane < (h + 1) * NA_DIM)
        qh = jnp.where(hm, q, zero)
        s_c = _dot(qh, kt_c)
        m = jnp.max(s_c, axis=-1, keepdims=True)
        if use_window:
            s_w = _dot(qh, kt_w) + bias_ref[h]
            m = jnp.maximum(m, jnp.max(s_w, axis=-1, keepdims=True))
            p_w = jnp.exp2(s_w - m)
        p_c = jnp.exp2(s_c - m)
        l = jnp.sum(p_c, axis=-1, keepdims=True)
        o = _dot(p_c.astype(BF16), v_c)
        if use_window:
            l = l + jnp.sum(p_w, axis=-1, keepdims=True)
            o = o + _dot(p_w.astype(BF16), v_w)
        out = jnp.where(hm, o / l, out)
    o_ref[...] = out.astype(o_ref.dtype)


def _na_bias_index(rows):
    nb = rows // NA_Q_ROWS
    assert rows % NA_Q_ROWS == 0 and nb >= 3 and rows >= NA_WIN_ROWS and (rows - NA_WIN_ROWS) % 2 == 0

    def block(i):
        r = i * NA_Q_ROWS + np.arange(NA_Q_ROWS)
        c = np.arange(GRID_W)
        kr = int(np.clip(i * NA_Q_ROWS - NA_KH // 2, 0, rows - NA_WIN_ROWS)) + np.arange(NA_WIN_ROWS)
        kc = np.arange(GRID_W)
        rs = np.clip(r - NA_KH // 2, 0, rows - NA_KH)
        cs = np.clip(c - NA_KW // 2, 0, GRID_W - NA_KW)
        vr = (kr[None, :] >= rs[:, None]) & (kr[None, :] < rs[:, None] + NA_KH)
        vc = (kc[None, :] >= cs[:, None]) & (kc[None, :] < cs[:, None] + NA_KW)
        valid = vr[:, None, :, None] & vc[None, :, None, :]
        dy = np.clip(kr[None, :] - r[:, None] + NA_KH - 1, 0, 2 * NA_KH - 2)
        dx = np.clip(kc[None, :] - c[:, None] + NA_KW - 1, 0, 2 * NA_KW - 2)
        dy = np.broadcast_to(dy[:, None, :, None], valid.shape)
        dx = np.broadcast_to(dx[None, :, None, :], valid.shape)
        n_q, n_k = NA_Q_ROWS * GRID_W, NA_WIN_ROWS * GRID_W
        return valid.reshape(n_q, n_k), dy.reshape(n_q, n_k), dx.reshape(n_q, n_k)

    blocks = [block(i) for i in range(nb)]
    for i in range(2, nb - 1):
        assert all(np.array_equal(a, b_) for a, b_ in zip(blocks[1], blocks[i]))
    cls = [blocks[0], blocks[1], blocks[nb - 1]]
    return tuple(np.stack([c[k] for c in cls]) for k in range(3))


def _na_bias(rpb, rows):
    valid, dy, dx = _na_bias_index(rows)
    vals = rpb.astype(F32)[:, dy, dx] * LOG2E
    return jnp.transpose(jnp.where(valid[None], vals, NEG_BIG), (1, 0, 2, 3))


def _neighbourhood_attention(nq, nkt, nv, bias, seq):
    b, t, w = nq.shape
    tq = NA_Q_ROWS * GRID_W
    nb = seq // tq
    rows = seq // GRID_W
    return pl.pallas_call(
        functools.partial(_na_kernel, rows=rows, ctx_off=seq, n_ctx=t - seq, use_window=True),
        grid=(b, nb),
        in_specs=[pl.BlockSpec((None, tq, w), lambda i, j: (i, j, 0)),
                  pl.BlockSpec((None, w, t), lambda i, j: (i, 0, 0)),
                  pl.BlockSpec((None, t, w), lambda i, j: (i, 0, 0)),
                  pl.BlockSpec((None,) + bias.shape[1:],
                               lambda i, j: (jnp.where(j == 0, 0, jnp.where(j == nb - 1, 2, 1)), 0, 0, 0))],
        out_specs=pl.BlockSpec((None, tq, w), lambda i, j: (i, j, 0)),
        out_shape=jax.ShapeDtypeStruct((b, t, w), BF16),
        compiler_params=_cparams(("parallel", "parallel")),
        name="attn_na",
    )(nq, nkt, nv, bias)


def _neighbourhood_attention_ctx(nq, nkt, nv, out, seq):
    b, t, w = nq.shape
    n_ctx = t - seq
    blk = seq // n_ctx
    kern = lambda q, k, v, _, o, **kw: _na_kernel(q, k, v, o, **kw)
    return pl.pallas_call(
        functools.partial(kern, rows=0, ctx_off=0, n_ctx=n_ctx, use_window=False),
        grid=(b,),
        in_specs=[pl.BlockSpec((None, n_ctx, w), lambda i: (i, blk, 0)),
                  pl.BlockSpec((None, w, n_ctx), lambda i: (i, 0, blk)),
                  pl.BlockSpec((None, n_ctx, w), lambda i: (i, blk, 0)),
                  pl.BlockSpec(memory_space=pl.ANY)],
        out_specs=pl.BlockSpec((None, n_ctx, w), lambda i: (i, blk, 0)),
        out_shape=jax.ShapeDtypeStruct(out.shape, out.dtype),
        input_output_aliases={3: 0},
        compiler_params=_cparams(("parallel",)),
        name="attn_na_ctx",
    )(nq, nkt, nv, out)


def _layer_norm(z, g, b):
    mu = jnp.mean(z, axis=-1, keepdims=True)
    zc = z - mu
    var = jnp.mean(jnp.square(zc), axis=-1, keepdims=True)
    return zc * lax.rsqrt(var + LN_EPS) * g + b


def _ffn_kernel(x_ref, ya_ref, yb_ref, yc_ref, mod_ref, wo_ref, w1_ref, w2_ref, ln_ref, o_ref, *, alpha):
    x = x_ref[...]
    a = (_dot(ya_ref[...], wo_ref[0:512, :]) + _dot(yb_ref[...], wo_ref[512:768, :])
         + _dot(yc_ref[...], wo_ref[768:1024, :]))
    x1 = _layer_norm(alpha * x + mod_ref[2:3, :] * a, ln_ref[0:1, :], ln_ref[1:2, :])
    h2 = (x1 * (1.0 + mod_ref[4:5, :]) + mod_ref[3:4, :]).astype(BF16)
    u = jnp.maximum(_dot(h2, w1_ref[...]), 0.0)
    f = _dot(jnp.square(u).astype(BF16), w2_ref[...])
    o_ref[...] = _layer_norm(alpha * x1 + mod_ref[5:6, :] * f, ln_ref[2:3, :], ln_ref[3:4, :])


def _out_ffn(xall, ya, yb, yc, mod_l, wo, w1, w2, ln, n_tok, n_ctx_row, alpha):
    b, t, d = xall.shape
    tm = TOK_TILE
    nt_all = t // tm
    nt = n_tok // tm

    def tok(width):
        return pl.BlockSpec((None, tm, width), lambda i, j: (i, j, 0))

    def const(arr):
        return pl.BlockSpec(arr.shape, lambda i, j: (0,) * arr.ndim, pipeline_mode=pl.Buffered(1))

    return pl.pallas_call(
        functools.partial(_ffn_kernel, alpha=alpha),
        grid=(b, nt),
        in_specs=[tok(d), tok(512), tok(256), tok(256),
                  pl.BlockSpec((None, 6, d), lambda i, j: (jnp.where(j == nt_all - 1, n_ctx_row, i), 0, 0)),
                  const(wo), const(w1), const(w2), const(ln)],
        out_specs=tok(d),
        out_shape=jax.ShapeDtypeStruct((b, n_tok, d), F32),
        compiler_params=_cparams(("parallel", "parallel")),
        name="out_ffn",
    )(xall, ya, yb, yc, mod_l, wo, w1, w2, ln)


def _rope_tables(seq, n_ctx):
    t = jnp.arange(seq, dtype=jnp.int32)
    row = (t // GRID_W).astype(F32)
    col = (t % GRID_W).astype(F32)

    def axial(n):
        half = n // 2
        inv = ROPE_BASE ** (-2.0 * jnp.arange(half, dtype=F32) / n)
        ar, ac = row[:, None] * inv[None, :], col[:, None] * inv[None, :]
        cos = jnp.concatenate([jnp.cos(ar), jnp.cos(ar), jnp.cos(ac), jnp.cos(ac)], axis=-1)
        sin = jnp.concatenate([-jnp.sin(ar), jnp.sin(ar), -jnp.sin(ac), jnp.sin(ac)], axis=-1)
        cos = jnp.concatenate([cos, jnp.ones((n_ctx, 2 * n), F32)], axis=0)
        sin = jnp.concatenate([sin, jnp.zeros((n_ctx, 2 * n), F32)], axis=0)
        return cos, sin

    cos_a, sin_a = axial(DA_DIM // 2)
    cos_m, sin_m = axial(MLA_ROPE // 2)
    sa = DA_DIM ** -0.5 * LOG2E
    sm = (MLA_NOPE + MLA_ROPE) ** -0.5 * LOG2E
    tt = seq + n_ctx
    ones = jnp.ones((tt, MLA_NOPE), F32)
    zpad = jnp.zeros((tt, HEAD_PAD - MLA_NOPE - MLA_ROPE), F32)
    return dict(
        ca=jnp.tile(cos_a, (1, 2)) * sa, sa=jnp.tile(sin_a, (1, 2)) * sa,
        cat=cos_a.T, sat=sin_a.T,
        cm=jnp.concatenate([ones, cos_m, zpad], axis=-1) * sm,
        sm=jnp.concatenate([0.0 * ones, sin_m, zpad], axis=-1) * sm,
        cmt=cos_m.T, smt=sin_m.T)


def _layer_weights(w_in, gq, w_uq, gkv, w_ukv):
    o = np.cumsum([0, 512, 512, 512, MLA_Q_RANK, MLA_KV_RANK, MLA_ROPE, 256, 256, 256])
    col = lambda k: w_in[:, int(o[k]):int(o[k + 1])]
    aq, ak, av, cq, ckv, kr, nq, nk, nv = (col(k) for k in range(9))
    wtok = jnp.concatenate([aq, av, cq, ckv, nq, nv], axis=1).astype(BF16)
    wt = jnp.concatenate([ak, ckv, kr, nk], axis=1).T.astype(BF16)
    uq = w_uq.reshape(MLA_Q_RANK, N_HEADS, MLA_NOPE + MLA_ROPE)
    uq = jnp.pad(uq, ((0, 0), (0, 0), (0, HEAD_PAD - MLA_NOPE - MLA_ROPE)))
    ukv = w_ukv.reshape(MLA_KV_RANK, N_HEADS, MLA_NOPE + MLA_VDIM)
    return dict(
        wtok=wtok, wt=wt,
        gq=gq.reshape(1, -1).astype(F32), wuq=uq.reshape(MLA_Q_RANK, N_HEADS * HEAD_PAD).astype(BF16),
        gkv=gkv.reshape(1, -1).astype(F32), gkvt=gkv.reshape(-1, 1).astype(F32),
        wuv=ukv[:, :, MLA_NOPE:].reshape(MLA_KV_RANK, N_HEADS * MLA_VDIM).astype(BF16),
        wukt=ukv[:, :, :MLA_NOPE].reshape(MLA_KV_RANK, N_HEADS * MLA_NOPE).T.astype(BF16))


def kernel(x, c, ctx, c_ctx, w_mod, b_mod, w_in, da_lam_q1, da_lam_k1, da_lam_q2, da_lam_k2, da_subln_g,
           mla_q_norm_g, mla_w_uq, mla_kv_norm_g, mla_w_ukv, na_rpb, w_out, ln1_g, ln1_b, w_ff1, w_ff2,
           ln2_g, ln2_b):
    b, seq, d = x.shape
    n_ctx = ctx.shape[1]
    depth = w_mod.shape[0]
    assert seq % TOK_TILE == 0 and n_ctx == TOK_TILE and b < 8 and seq % (NA_Q_ROWS * GRID_W) == 0
    alpha = (2.0 * depth) ** 0.25
    tq, tk = 256, 512
    assert seq % tq == 0 and seq % tk == 0

    cs = jnp.zeros((8, d), F32).at[:b].set(c).at[b].set(c_ctx)
    mod = _modulation(cs, w_mod, b_mod).reshape(depth, 8, 6, d)
    tabs = _rope_tables(seq, n_ctx)
    xall = jnp.concatenate([x, ctx], axis=1)
    ctx_blk = seq // n_ctx

    for l in range(depth):
        last = l == depth - 1
        lam_init = 0.8 - 0.6 * math.exp(-0.3 * l)
        pw = _layer_weights(w_in[l], mla_q_norm_g[l], mla_w_uq[l], mla_kv_norm_g[l], mla_w_ukv[l])
        qa, kat, va, qm, kmt, vm, nq, nkt, nv = _projection(xall, mod[l], pw, tabs, b)

        lam_vec = jnp.stack([da_lam_q1[l], da_lam_k1[l], da_lam_q2[l], da_lam_k2[l]]).astype(F32)
        g_sub = da_subln_g[l].reshape(1, -1).astype(F32)
        ya = _dense_attention("da", qa, kat, va, tq=tq, tk=tk, q_blk0=0, k_blk0=0, n_k=seq + n_ctx,
                              out=seq, extra=(lam_vec, g_sub), lam_init=lam_init)
        yb = _dense_attention("mla", qm, kmt, vm, tq=tq, tk=tk, q_blk0=0, k_blk0=0, n_k=seq + n_ctx, out=seq)
        yc = _neighbourhood_attention(nq, nkt, nv, _na_bias(na_rpb[l], seq // GRID_W), seq)
        if not last:
            ya = _dense_attention("da", qa, kat, va, tq=n_ctx, tk=n_ctx, q_blk0=ctx_blk, k_blk0=ctx_blk,
                                  n_k=n_ctx, out=ya, extra=(lam_vec, g_sub), lam_init=lam_init)
            yb = _dense_attention("mla", qm, kmt, vm, tq=n_ctx, tk=n_ctx, q_blk0=ctx_blk, k_blk0=ctx_blk,
                                  n_k=n_ctx, out=yb)
            yc = _neighbourhood_attention_ctx(nq, nkt, nv, yc, seq)

        ln = jnp.stack([ln1_g[l], ln1_b[l], ln2_g[l], ln2_b[l]]).astype(F32)
        xall = _out_ffn(xall, ya, yb, yc, mod[l], w_out[l].astype(BF16), w_ff1[l].astype(BF16),
                        w_ff2[l].astype(BF16), ln, seq if last else seq + n_ctx, b, alpha)
    return xall
```

```python
import functools
import math

import numpy as np
import jax
import jax.numpy as jnp
from jax import lax
from jax.experimental import pallas as pl
from jax.experimental.pallas import tpu as pltpu

F32 = jnp.float32
BF16 = jnp.bfloat16

GRID_W = 64
ROPE_BASE = 10000.0
LN_EPS = 1e-6
RMS_EPS = 1e-6
SUBLN_EPS = 1e-5
N_HEADS = 4
DA_DIM = 64
DA_VDIM = 128
MLA_Q_RANK = 256
MLA_KV_RANK = 128
MLA_NOPE = 64
MLA_ROPE = 32
MLA_VDIM = 64
NA_DIM = 64
NA_KH = 8
NA_KW = 16
HEAD_PAD = 128
MLA_ZPAD = HEAD_PAD - MLA_NOPE - MLA_ROPE

LOG2E = math.log2(math.e)
NEG_BIG = -1e30

TOK_TILE = 256
ATTN_TQ = 256
ATTN_TK = 1024
NA_Q_ROWS = 4
NA_WIN_ROWS = NA_Q_ROWS + NA_KH
VMEM_LIMIT = 56 * 1024 * 1024


def _cparams(sem):
    return pltpu.CompilerParams(dimension_semantics=sem, vmem_limit_bytes=VMEM_LIMIT)


def _dot(a, b):
    return jnp.dot(a, b, preferred_element_type=F32)


def _dot_exact(a, b):
    return jnp.dot(a, b, preferred_element_type=F32, precision=lax.Precision.HIGHEST)


def _mod_kernel(c_ref, w_ref, b_ref, o_ref):
    c = c_ref[...]
    s = c / (1.0 + jnp.exp(-c))
    o_ref[...] = _dot_exact(s, w_ref[...]) + b_ref[...]


def _modulation(cs, w_mod, b_mod):
    depth, d, n = w_mod.shape
    nb = 1024
    return pl.pallas_call(
        _mod_kernel,
        grid=(depth, n // nb),
        in_specs=[pl.BlockSpec((8, d), lambda l, j: (0, 0)),
                  pl.BlockSpec((None, d, nb), lambda l, j: (l, 0, j)),
                  pl.BlockSpec((None, 1, nb), lambda l, j: (l, 0, j))],
        out_specs=pl.BlockSpec((None, 8, nb), lambda l, j: (l, 0, j)),
        out_shape=jax.ShapeDtypeStruct((depth, 8, n), F32),
        compiler_params=_cparams(("arbitrary", "arbitrary")),
        name="modulation",
    )(cs, w_mod, b_mod.reshape(depth, 1, n))


def _rms(x, axis, eps):
    return x * lax.rsqrt(jnp.mean(jnp.square(x), axis=axis, keepdims=True) + eps)


def _rope_lanes(x, lane, half, cos, sin):
    partner = jnp.where((lane & half) == 0, pltpu.roll(x, HEAD_PAD - half, 1), pltpu.roll(x, half, 1))
    return x * cos + partner * sin


def _rope_rows(x, half, cos, sin):
    partner = jnp.concatenate([x[half:2 * half], x[0:half], x[3 * half:4 * half], x[2 * half:3 * half]], axis=0)
    return x * cos + partner * sin


def _proj_kernel(x_ref, mod_ref, wtok_ref, wt_ref,
                 caq_ref, saq_ref, cak_ref, sak_ref, cmq_ref, smq_ref, cmk_ref, smk_ref,
                 gqt_ref, wuqt_ref, gkv_ref, wuk_ref, gkvt_ref, wuvt_ref,
                 qat_ref, ka_ref, vat_ref, qmt_ref, km_ref, vmt_ref, nq_ref, nkt_ref, nv_ref):
    x = x_ref[...]
    h = (x * (1.0 + mod_ref[1:2, :]) + mod_ref[0:1, :]).astype(BF16)
    p = _dot(h, wtok_ref[...])
    pt = lax.dot_general(wt_ref[...], h, (((1,), (1,)), ((), ())),
                         preferred_element_type=F32)
    tm = x.shape[0]
    lane = lax.broadcasted_iota(jnp.int32, (tm, HEAD_PAD), 1)

    caq, saq = caq_ref[...], saq_ref[...]
    for g in range(2 * N_HEADS):
        rows = slice(g * DA_DIM, (g + 1) * DA_DIM)
        qat_ref[rows, :] = _rope_rows(pt[rows, :], DA_DIM // 4, caq, saq).astype(BF16)
    cak, sak = cak_ref[...], sak_ref[...]
    for g in range(N_HEADS):
        cols = slice(g * HEAD_PAD, (g + 1) * HEAD_PAD)
        ka_ref[:, cols] = _rope_lanes(p[:, cols], lane, DA_DIM // 4, cak, sak).astype(BF16)
    vat_ref[...] = pt[512:1024, :].astype(BF16)

    cqnt = (_rms(pt[1024:1280, :], 0, RMS_EPS) * gqt_ref[...]).astype(BF16)
    qmt = _dot(wuqt_ref[...], cqnt)
    cmq, smq = cmq_ref[...], smq_ref[...]
    scale_m = (MLA_NOPE + MLA_ROPE) ** -0.5 * LOG2E
    for g in range(N_HEADS):
        r0 = g * HEAD_PAD
        qmt_ref[r0:r0 + MLA_NOPE, :] = (qmt[r0:r0 + MLA_NOPE, :] * scale_m).astype(BF16)
        qmt_ref[r0 + MLA_NOPE:r0 + MLA_NOPE + MLA_ROPE, :] = _rope_rows(
            qmt[r0 + MLA_NOPE:r0 + MLA_NOPE + MLA_ROPE, :], MLA_ROPE // 4, cmq, smq).astype(BF16)
        qmt_ref[r0 + MLA_NOPE + MLA_ROPE:r0 + HEAD_PAD, :] = jnp.zeros((MLA_ZPAD, tm), BF16)
    ckvn = (_rms(p[:, 1024:1152], -1, RMS_EPS) * gkv_ref[...]).astype(BF16)
    km = _dot(ckvn, wuk_ref[...]) + p[:, 512:1024]
    cmk, smk = cmk_ref[...], smk_ref[...]
    for g in range(N_HEADS):
        cols = slice(g * HEAD_PAD, (g + 1) * HEAD_PAD)
        km_ref[:, cols] = _rope_lanes(km[:, cols], lane, MLA_ROPE // 4, cmk, smk).astype(BF16)
    ckvnt = (_rms(pt[1280:1408, :], 0, RMS_EPS) * gkvt_ref[...]).astype(BF16)
    vmt_ref[...] = _dot(wuvt_ref[...], ckvnt).astype(BF16)

    nq_ref[...] = (p[:, 1152:1408] * (NA_DIM ** -0.5 * LOG2E)).astype(BF16)
    nv_ref[...] = p[:, 1408:1664].astype(BF16)
    nkt_ref[...] = pt[1408:1664, :].astype(BF16)


def _projection(xall, mod_l, pw, tabs, n_ctx_row):
    b, t, d = xall.shape
    tm = TOK_TILE
    nt = t // tm

    def tok(width):
        return pl.BlockSpec((None, tm, width), lambda j, i: (i, j, 0))

    def trn(height):
        return pl.BlockSpec((None, height, tm), lambda j, i: (i, 0, j))

    def const(arr):
        return pl.BlockSpec(arr.shape, lambda j, i: (0,) * arr.ndim)

    def tab_tok():
        return pl.BlockSpec((tm, HEAD_PAD), lambda j, i: (j, 0))

    def tab_trn(height):
        return pl.BlockSpec((height, tm), lambda j, i: (0, j))

    consts = [pw["gqt"], pw["wuqt"], pw["gkv"], pw["wuk"], pw["gkvt"], pw["wuvt"]]
    in_specs = [tok(d),
                pl.BlockSpec((None, 6, d), lambda j, i: (jnp.where(j == nt - 1, n_ctx_row, i), 0, 0)),
                const(pw["wtok"]), const(pw["wt"]),
                tab_trn(64), tab_trn(64), tab_tok(), tab_tok(),
                tab_trn(32), tab_trn(32), tab_tok(), tab_tok(),
                ] + [const(a) for a in consts]
    out_specs = [trn(512), tok(512), trn(512), trn(512), tok(512), trn(256), tok(256), trn(256), tok(256)]
    out_shape = [jax.ShapeDtypeStruct((b, 512, t), BF16), jax.ShapeDtypeStruct((b, t, 512), BF16),
                 jax.ShapeDtypeStruct((b, 512, t), BF16), jax.ShapeDtypeStruct((b, 512, t), BF16),
                 jax.ShapeDtypeStruct((b, t, 512), BF16), jax.ShapeDtypeStruct((b, 256, t), BF16),
                 jax.ShapeDtypeStruct((b, t, 256), BF16), jax.ShapeDtypeStruct((b, 256, t), BF16),
                 jax.ShapeDtypeStruct((b, t, 256), BF16)]
    return pl.pallas_call(
        _proj_kernel,
        grid=(nt, b),
        in_specs=in_specs,
        out_specs=out_specs,
        out_shape=out_shape,
        compiler_params=_cparams(("parallel", "parallel")),
        name="projection",
    )(xall, mod_l, pw["wtok"], pw["wt"],
      tabs["caq"], tabs["saq"], tabs["cak"], tabs["sak"], tabs["cmq"], tabs["smq"], tabs["cmk"], tabs["smk"],
      *consts)


def _tree_reduce(op, x):
    n = x.shape[0]
    while n > 8 and n % 16 == 0:
        n //= 2
        x = op(x[:n], x[n:])
    red = jnp.max if op is jnp.maximum else jnp.sum
    return red(x, axis=0, keepdims=True)


def _attn_kernel(*refs, mode, n_k, tk, lam_init, aliased_out):
    o_ref, m_ref, l_ref, acc_ref, s_ref = refs[-5:]
    refs = refs[:-6] if aliased_out else refs[:-5]
    if mode == "da":
        qt_ref, k_ref, vt_ref, lam_ref, g_ref = refs
        qt = qt_ref[...]
        zero = jnp.zeros((DA_DIM, qt.shape[1]), qt.dtype)
        qts = (jnp.concatenate([qt[:DA_DIM], zero], axis=0), jnp.concatenate([zero, qt[DA_DIM:]], axis=0))
        kcols = ((0, 128), (0, 128))
    else:
        qt_ref, k_ref, vt_ref = refs
        qts = (qt_ref[0:128, :], qt_ref[128:256, :])
        kcols = ((0, 128), (128, 256))

    m_ref[...] = jnp.full(m_ref.shape, NEG_BIG, F32)
    l_ref[...] = jnp.zeros(l_ref.shape, F32)
    acc_ref[...] = jnp.zeros(acc_ref.shape, F32)

    def scores(start, size):
        return [_dot(k_ref[pl.ds(start, size), kcols[i][0]:kcols[i][1]], qts[i]) for i in range(2)]

    def scores_into(slot, start):
        for i, st in enumerate(scores(start, tk)):
            s_ref[slot, i] = st

    def softmax_pv(sts, start, size):
        vt_c = vt_ref[:, pl.ds(start, size)]
        ps = []
        for i in range(2):
            st = sts[i]
            m_prev = m_ref[i]
            m_new = jnp.maximum(m_prev, _tree_reduce(jnp.maximum, st))
            p = jnp.exp2(st - m_new)
            alpha = jnp.exp2(m_prev - m_new)
            l_ref[i] = alpha * l_ref[i] + _tree_reduce(jnp.add, p)
            m_ref[i] = m_new
            ps.append((alpha, p.astype(BF16)))
        for i in range(2):
            acc_ref[i] = ps[i][0] * acc_ref[i] + _dot(vt_c, ps[i][1])

    def from_slot(slot):
        return [s_ref[slot, 0], s_ref[slot, 1]]

    n_main, tail = n_k // tk, n_k % tk
    n_pairs = (n_main - 1) // 2
    scores_into(0, 0)

    def body(j, carry):
        c0 = pl.multiple_of(2 * j * tk, 2 * tk)
        scores_into(1, c0 + tk)
        softmax_pv(from_slot(0), c0, tk)
        scores_into(0, c0 + 2 * tk)
        softmax_pv(from_slot(1), c0 + tk, tk)
        return carry

    lax.fori_loop(0, n_pairs, body, 0)
    c0 = 2 * n_pairs * tk
    if n_main - 2 * n_pairs == 2:
        scores_into(1, c0 + tk)
        softmax_pv(from_slot(0), c0, tk)
        tail_scores = scores(n_main * tk, tail) if tail else None
        softmax_pv(from_slot(1), c0 + tk, tk)
    else:
        tail_scores = scores(n_main * tk, tail) if tail else None
        softmax_pv(from_slot(0), c0, tk)
    if tail:
        softmax_pv(tail_scores, n_main * tk, tail)

    o0 = acc_ref[0] / l_ref[0]
    o1 = acc_ref[1] / l_ref[1]
    if mode == "da":
        lv = lam_ref[...]
        lam = (jnp.exp(jnp.sum(lv[0:1] * lv[1:2], axis=-1, keepdims=True))
               - jnp.exp(jnp.sum(lv[2:3] * lv[3:4], axis=-1, keepdims=True)) + lam_init)
        o = o0 - lam * o1
        o = _rms(o, 0, SUBLN_EPS) * g_ref[...] * (1.0 - lam_init)
    else:
        o = jnp.concatenate([o0[:MLA_VDIM], o1[MLA_VDIM:]], axis=0)
    o_ref[...] = o.T.astype(o_ref.dtype)


def _dense_attention(mode, qt, k, vt, *, tq, tk, q_blk0, k_blk0, n_k, out, extra=(), lam_init=0.0):
    b, _, t = qt.shape
    groups = N_HEADS if mode == "da" else N_HEADS // 2
    qw = 128 if mode == "da" else 256
    aliased = not isinstance(out, int)
    if aliased:
        n_q, out_shape = tq, jax.ShapeDtypeStruct(out.shape, out.dtype)
    else:
        n_q, out_shape = out, jax.ShapeDtypeStruct((b, t, 128 * groups), BF16)
    in_specs = [pl.BlockSpec((None, qw, tq), lambda i, g, j: (i, g, q_blk0 + j)),
                pl.BlockSpec((None, n_k, qw), lambda i, g, j: (i, k_blk0, g)),
                pl.BlockSpec((None, 128, n_k), lambda i, g, j: (i, g, k_blk0))]
    args = [qt, k, vt]
    for a in extra:
        in_specs.append(pl.BlockSpec(a.shape, lambda i, g, j: (0, 0)))
        args.append(a)
    alias = {}
    if aliased:
        in_specs.append(pl.BlockSpec(memory_space=pl.ANY))
        alias = {len(args): 0}
        args.append(out)
    return pl.pallas_call(
        functools.partial(_attn_kernel, mode=mode, n_k=n_k, tk=tk, lam_init=lam_init, aliased_out=aliased),
        grid=(b, groups, n_q // tq),
        in_specs=in_specs,
        out_specs=pl.BlockSpec((None, tq, 128), lambda i, g, j: (i, q_blk0 + j, g)),
        out_shape=out_shape,
        scratch_shapes=[pltpu.VMEM((2, 1, tq), F32), pltpu.VMEM((2, 1, tq), F32),
                        pltpu.VMEM((2, 128, tq), F32), pltpu.VMEM((2, 2, tk, tq), F32)],
        input_output_aliases=alias,
        compiler_params=_cparams(("parallel", "parallel", "parallel")),
        name="attn_" + mode,
    )(*args)


def _na_kernel(*refs, rows, ctx_off, n_ctx, use_window):
    if use_window:
        q_ref, kt_ref, v_ref, bias_ref, o_ref = refs
    else:
        q_ref, kt_ref, v_ref, o_ref = refs
    q = q_ref[...]
    lane = lax.broadcasted_iota(jnp.int32, q.shape, 1)
    zero = jnp.zeros_like(q)
    kt_c = kt_ref[:, ctx_off:ctx_off + n_ctx]
    v_c = v_ref[ctx_off:ctx_off + n_ctx, :]
    if use_window:
        i = pl.program_id(1)
        kr0 = jnp.clip(i * NA_Q_ROWS - NA_KH // 2, 0, rows - NA_WIN_ROWS)
        start = pl.multiple_of(kr0 * GRID_W, 128)
        kt_w = kt_ref[:, pl.ds(start, NA_WIN_ROWS * GRID_W)]
        v_w = v_ref[pl.ds(start, NA_WIN_ROWS * GRID_W), :]
    out = jnp.zeros(q.shape, F32)
    for h in range(N_HEADS):
        hm = (lane >= h * NA_DIM) & (lane < (h + 1) * NA_DIM)
        qh = jnp.where(hm, q, zero)
        s_c = _dot(qh, kt_c)
        m = jnp.max(s_c, axis=-1, keepdims=True)
        if use_window:
            s_w = _dot(qh, kt_w) + bias_ref[h]
            m = jnp.maximum(m, jnp.max(s_w, axis=-1, keepdims=True))
            p_w = jnp.exp2(s_w - m)
        p_c = jnp.exp2(s_c - m)
        l = jnp.sum(p_c, axis=-1, keepdims=True)
        o = _dot(p_c.astype(BF16), v_c)
        if use_window:
            l = l + jnp.sum(p_w, axis=-1, keepdims=True)
            o = o + _dot(p_w.astype(BF16), v_w)
        out = jnp.where(hm, o / l, out)
    o_ref[...] = out.astype(o_ref.dtype)


def _na_bias_index(rows):
    nb = rows // NA_Q_ROWS
    assert rows % NA_Q_ROWS == 0 and nb >= 3 and rows >= NA_WIN_ROWS and (rows - NA_WIN_ROWS) % 2 == 0

    def block(i):
        r = i * NA_Q_ROWS + np.arange(NA_Q_ROWS)
        kr = int(np.clip(i * NA_Q_ROWS - NA_KH // 2, 0, rows - NA_WIN_ROWS)) + np.arange(NA_WIN_ROWS)
        rs = np.clip(r - NA_KH // 2, 0, rows - NA_KH)
        valid = (kr[None, :] >= rs[:, None]) & (kr[None, :] < rs[:, None] + NA_KH)
        dy = np.clip(kr[None, :] - r[:, None] + NA_KH - 1, 0, 2 * NA_KH - 2)
        return valid, dy

    blocks = [block(i) for i in range(nb)]
    for i in range(2, nb - 1):
        assert all(np.array_equal(a, b_) for a, b_ in zip(blocks[1], blocks[i]))
    cls = [blocks[0], blocks[1], blocks[nb - 1]]
    rvalid = np.stack([c[0] for c in cls]).reshape(-1)
    dy = np.stack([c[1] for c in cls]).reshape(-1)
    rsel = (dy[:, None] == np.arange(2 * NA_KH - 1)[None, :]).astype(np.float32)

    c = np.arange(GRID_W)
    cs = np.clip(c - NA_KW // 2, 0, GRID_W - NA_KW)
    cvalid = ((c[None, :] >= cs[:, None]) & (c[None, :] < cs[:, None] + NA_KW)).reshape(-1)
    dx = np.clip(c[None, :] - c[:, None] + NA_KW - 1, 0, 2 * NA_KW - 2).reshape(-1)
    csel = (np.arange(2 * NA_KW - 1)[:, None] == dx[None, :]).astype(np.float32)
    return rsel, rvalid.astype(np.float32)[:, None], csel, cvalid.astype(np.float32)[None, :]


def _na_bias_kernel(rpb_ref, rsel_ref, rmask_ref, csel_ref, cmask_ref, o_ref):
    cols = _dot_exact(rpb_ref[...], csel_ref[...])
    vals = _dot_exact(rsel_ref[...], cols)
    valid = (rmask_ref[...] * cmask_ref[...]) > 0.5
    o_ref[...] = jnp.where(valid, vals * LOG2E, NEG_BIG)


def _na_bias(rpb, rows):
    rsel, rmask, csel, cmask = (jnp.asarray(a) for a in _na_bias_index(rows))
    nh = rpb.shape[0]
    n_r, n_c = rsel.shape[0], csel.shape[1]
    full = lambda a: pl.BlockSpec(a.shape, lambda h: (0, 0))
    out = pl.pallas_call(
        _na_bias_kernel,
        grid=(nh,),
        in_specs=[pl.BlockSpec((None,) + rpb.shape[1:], lambda h: (h, 0, 0)),
                  full(rsel), full(rmask), full(csel), full(cmask)],
        out_specs=pl.BlockSpec((None, n_r, n_c), lambda h: (h, 0, 0)),
        out_shape=jax.ShapeDtypeStruct((nh, n_r, n_c), F32),
        compiler_params=_cparams(("parallel",)),
        name="na_bias",
    )(rpb.astype(F32), rsel, rmask, csel, cmask)
    out = out.reshape(nh, 3, NA_Q_ROWS, NA_WIN_ROWS, GRID_W, GRID_W)
    out = jnp.transpose(out, (1, 0, 2, 4, 3, 5))
    return out.reshape(3, nh, NA_Q_ROWS * GRID_W, NA_WIN_ROWS * GRID_W)


def _neighbourhood_attention(nq, nkt, nv, bias, seq):
    b, t, w = nq.shape
    tq = NA_Q_ROWS * GRID_W
    nb = seq // tq
    rows = seq // GRID_W
    return pl.pallas_call(
        functools.partial(_na_kernel, rows=rows, ctx_off=seq, n_ctx=t - seq, use_window=True),
        grid=(b, nb),
        in_specs=[pl.BlockSpec((None, tq, w), lambda i, j: (i, j, 0)),
                  pl.BlockSpec((None, w, t), lambda i, j: (i, 0, 0)),
                  pl.BlockSpec((None, t, w), lambda i, j: (i, 0, 0)),
                  pl.BlockSpec((None,) + bias.shape[1:],
                               lambda i, j: (jnp.where(j == 0, 0, jnp.where(j == nb - 1, 2, 1)), 0, 0, 0))],
        out_specs=pl.BlockSpec((None, tq, w), lambda i, j: (i, j, 0)),
        out_shape=jax.ShapeDtypeStruct((b, t, w), BF16),
        compiler_params=_cparams(("parallel", "parallel")),
        name="attn_na",
    )(nq, nkt, nv, bias)


def _neighbourhood_attention_ctx(nq, nkt, nv, out, seq):
    b, t, w = nq.shape
    n_ctx = t - seq
    blk = seq // n_ctx
    kern = lambda q, k, v, _, o, **kw: _na_kernel(q, k, v, o, **kw)
    return pl.pallas_call(
        functools.partial(kern, rows=0, ctx_off=0, n_ctx=n_ctx, use_window=False),
        grid=(b,),
        in_specs=[pl.BlockSpec((None, n_ctx, w), lambda i: (i, blk, 0)),
                  pl.BlockSpec((None, w, n_ctx), lambda i: (i, 0, blk)),
                  pl.BlockSpec((None, n_ctx, w), lambda i: (i, blk, 0)),
                  pl.BlockSpec(memory_space=pl.ANY)],
        out_specs=pl.BlockSpec((None, n_ctx, w), lambda i: (i, blk, 0)),
        out_shape=jax.ShapeDtypeStruct(out.shape, out.dtype),
        input_output_aliases={3: 0},
        compiler_params=_cparams(("parallel",)),
        name="attn_na_ctx",
    )(nq, nkt, nv, out)


def _layer_norm(z, g, b):
    mu = jnp.mean(z, axis=-1, keepdims=True)
    zc = z - mu
    var = jnp.mean(jnp.square(zc), axis=-1, keepdims=True)
    return zc * lax.rsqrt(var + LN_EPS) * g + b


def _ffn_kernel(x_ref, ya_ref, yb_ref, yc_ref, mod_ref, wo_ref, w1_ref, w2_ref, ln_ref, o_ref, *, alpha):
    x = x_ref[...]
    a = (_dot(ya_ref[...], wo_ref[0:512, :]) + _dot(yb_ref[...], wo_ref[512:768, :])
         + _dot(yc_ref[...], wo_ref[768:1024, :]))
    x1 = _layer_norm(alpha * x + mod_ref[2:3, :] * a, ln_ref[0:1, :], ln_ref[1:2, :])
    h2 = (x1 * (1.0 + mod_ref[4:5, :]) + mod_ref[3:4, :]).astype(BF16)
    u = jnp.maximum(_dot(h2, w1_ref[...]), 0.0)
    f = _dot(jnp.square(u).astype(BF16), w2_ref[...])
    o_ref[...] = _layer_norm(alpha * x1 + mod_ref[5:6, :] * f, ln_ref[2:3, :], ln_ref[3:4, :])


def _out_ffn(xall, ya, yb, yc, mod_l, wo, w1, w2, ln, n_tok, n_ctx_row, alpha):
    b, t, d = xall.shape
    tm = TOK_TILE
    nt_all = t // tm
    nt = n_tok // tm

    def tok(width):
        return pl.BlockSpec((None, tm, width), lambda i, j: (i, j, 0))

    def const(arr):
        return pl.BlockSpec(arr.shape, lambda i, j: (0,) * arr.ndim, pipeline_mode=pl.Buffered(1))

    return pl.pallas_call(
        functools.partial(_ffn_kernel, alpha=alpha),
        grid=(b, nt),
        in_specs=[tok(d), tok(512), tok(256), tok(256),
                  pl.BlockSpec((None, 6, d), lambda i, j: (jnp.where(j == nt_all - 1, n_ctx_row, i), 0, 0)),
                  const(wo), const(w1), const(w2), const(ln)],
        out_specs=tok(d),
        out_shape=jax.ShapeDtypeStruct((b, n_tok, d), F32),
        compiler_params=_cparams(("parallel", "parallel")),
        name="out_ffn",
    )(xall, ya, yb, yc, mod_l, wo, w1, w2, ln)


def _rope_tables(seq, n_ctx):
    t = jnp.arange(seq, dtype=jnp.int32)
    row = (t // GRID_W).astype(F32)
    col = (t % GRID_W).astype(F32)

    def axial(n):
        half = n // 2
        inv = ROPE_BASE ** (-2.0 * jnp.arange(half, dtype=F32) / n)
        ar, ac = row[:, None] * inv[None, :], col[:, None] * inv[None, :]
        cos = jnp.concatenate([jnp.cos(ar), jnp.cos(ar), jnp.cos(ac), jnp.cos(ac)], axis=-1)
        sin = jnp.concatenate([-jnp.sin(ar), jnp.sin(ar), -jnp.sin(ac), jnp.sin(ac)], axis=-1)
        cos = jnp.concatenate([cos, jnp.ones((n_ctx, 2 * n), F32)], axis=0)
        sin = jnp.concatenate([sin, jnp.zeros((n_ctx, 2 * n), F32)], axis=0)
        return cos, sin

    cos_a, sin_a = axial(DA_DIM // 2)
    cos_m, sin_m = axial(MLA_ROPE // 2)
    sa = DA_DIM ** -0.5 * LOG2E
    sm = (MLA_NOPE + MLA_ROPE) ** -0.5 * LOG2E
    tt = seq + n_ctx
    ones = jnp.ones((tt, MLA_NOPE), F32)
    zpad = jnp.zeros((tt, MLA_ZPAD), F32)
    return dict(
        caq=cos_a.T * sa, saq=sin_a.T * sa,
        cak=jnp.tile(cos_a, (1, 2)), sak=jnp.tile(sin_a, (1, 2)),
        cmq=cos_m.T * sm, smq=sin_m.T * sm,
        cmk=jnp.concatenate([ones, cos_m, zpad], axis=-1),
        smk=jnp.concatenate([0.0 * ones, sin_m, zpad], axis=-1))


def _layer_weights(w_in, gq, w_uq, gkv, w_ukv):
    d = w_in.shape[0]
    o = np.cumsum([0, 512, 512, 512, MLA_Q_RANK, MLA_KV_RANK, MLA_ROPE, 256, 256, 256])
    col = lambda k: w_in[:, int(o[k]):int(o[k + 1])]
    aq, ak, av, cq, ckv, kr, nq, nk, nv = (col(k) for k in range(9))
    kr_placed = jnp.concatenate([jnp.zeros((d, MLA_NOPE), w_in.dtype), kr,
                                 jnp.zeros((d, MLA_ZPAD), w_in.dtype)], axis=1)
    wtok = jnp.concatenate([ak] + [kr_placed] * N_HEADS + [ckv, nq, nv], axis=1).astype(BF16)
    wt = jnp.concatenate([aq, av, cq, ckv, nk], axis=1).T.astype(BF16)
    uq = w_uq.reshape(MLA_Q_RANK, N_HEADS, MLA_NOPE + MLA_ROPE)
    uq = jnp.pad(uq, ((0, 0), (0, 0), (0, MLA_ZPAD)))
    ukv = w_ukv.reshape(MLA_KV_RANK, N_HEADS, MLA_NOPE + MLA_VDIM)
    uk = jnp.pad(ukv[:, :, :MLA_NOPE], ((0, 0), (0, 0), (0, HEAD_PAD - MLA_NOPE)))
    return dict(
        wtok=wtok, wt=wt,
        gqt=gq.reshape(-1, 1).astype(F32),
        wuqt=uq.reshape(MLA_Q_RANK, N_HEADS * HEAD_PAD).T.astype(BF16),
        gkv=gkv.reshape(1, -1).astype(F32), gkvt=gkv.reshape(-1, 1).astype(F32),
        wuk=uk.reshape(MLA_KV_RANK, N_HEADS * HEAD_PAD).astype(BF16),
        wuvt=ukv[:, :, MLA_NOPE:].reshape(MLA_KV_RANK, N_HEADS * MLA_VDIM).T.astype(BF16))


def kernel(x, c, ctx, c_ctx, w_mod, b_mod, w_in, da_lam_q1, da_lam_k1, da_lam_q2, da_lam_k2, da_subln_g,
           mla_q_norm_g, mla_w_uq, mla_kv_norm_g, mla_w_ukv, na_rpb, w_out, ln1_g, ln1_b, w_ff1, w_ff2,
           ln2_g, ln2_b):
    b, seq, d = x.shape
    n_ctx = ctx.shape[1]
    depth = w_mod.shape[0]
    assert seq % TOK_TILE == 0 and n_ctx == TOK_TILE and b < 8 and seq % (NA_Q_ROWS * GRID_W) == 0
    alpha = (2.0 * depth) ** 0.25
    tq, tk = ATTN_TQ, ATTN_TK
    assert seq % tq == 0 and n_ctx % 128 == 0

    cs = jnp.zeros((8, d), F32).at[:b].set(c).at[b].set(c_ctx)
    mod = _modulation(cs, w_mod, b_mod).reshape(depth, 8, 6, d)
    tabs = _rope_tables(seq, n_ctx)
    xall = jnp.concatenate([x, ctx], axis=1)
    ctx_blk = seq // n_ctx

    for l in range(depth):
        last = l == depth - 1
        lam_init = 0.8 - 0.6 * math.exp(-0.3 * l)
        pw = _layer_weights(w_in[l], mla_q_norm_g[l], mla_w_uq[l], mla_kv_norm_g[l], mla_w_ukv[l])
        qat, ka, vat, qmt, km, vmt, nq, nkt, nv = _projection(xall, mod[l], pw, tabs, b)

        lam_vec = jnp.stack([da_lam_q1[l], da_lam_k1[l], da_lam_q2[l], da_lam_k2[l]]).astype(F32)
        g_sub = da_subln_g[l].reshape(-1, 1).astype(F32)
        ya = _dense_attention("da", qat, ka, vat, tq=tq, tk=tk, q_blk0=0, k_blk0=0, n_k=seq + n_ctx,
                              out=seq, extra=(lam_vec, g_sub), lam_init=lam_init)
        yb = _dense_attention("mla", qmt, km, vmt, tq=tq, tk=tk, q_blk0=0, k_blk0=0, n_k=seq + n_ctx, out=seq)
        yc = _neighbourhood_attention(nq, nkt, nv, _na_bias(na_rpb[l], seq // GRID_W), seq)
        if not last:
            ya = _dense_attention("da", qat, ka, vat, tq=n_ctx, tk=n_ctx, q_blk0=ctx_blk, k_blk0=ctx_blk,
                                  n_k=n_ctx, out=ya, extra=(lam_vec, g_sub), lam_init=lam_init)
            yb = _dense_attention("mla", qmt, km, vmt, tq=n_ctx, tk=n_ctx, q_blk0=ctx_blk, k_blk0=ctx_blk,
                                  n_k=n_ctx, out=yb)
            yc = _neighbourhood_attention_ctx(nq, nkt, nv, yc, seq)

        ln = jnp.stack([ln1_g[l], ln1_b[l], ln2_g[l], ln2_b[l]]).astype(F32)
        xall = _out_ffn(xall, ya, yb, yc, mod[l], w_out[l].astype(BF16), w_ff1[l].astype(BF16),
                        w_ff2[l].astype(BF16), ln, seq if last else seq + n_ctx, b, alpha)
    return xall
```

```python
import functools
import math

import numpy as np
import jax
import jax.numpy as jnp
from jax import lax
from jax.experimental import pallas as pl
from jax.experimental.pallas import tpu as pltpu

F32 = jnp.float32
BF16 = jnp.bfloat16

GRID_W = 64
ROPE_BASE = 10000.0
LN_EPS = 1e-6
RMS_EPS = 1e-6
SUBLN_EPS = 1e-5
N_HEADS = 4
DA_DIM = 64
DA_VDIM = 128
MLA_Q_RANK = 256
MLA_KV_RANK = 128
MLA_NOPE = 64
MLA_ROPE = 32
MLA_VDIM = 64
NA_DIM = 64
NA_KH = 8
NA_KW = 16
HEAD_PAD = 128
MLA_ZPAD = HEAD_PAD - MLA_NOPE - MLA_ROPE
ONES_ROWS = 16
DA_VROWS = DA_VDIM + ONES_ROWS
MLA_VROWS = MLA_VDIM + ONES_ROWS

LOG2E = math.log2(math.e)
NEG_BIG = -1e30

TOK_TILE = 256
ATTN_TQ = 256
ATTN_TK = 1024
BODY_STAGES = 2
FINE = 128
ROW_BLOCK = 64
NA_Q_ROWS = 4
NA_WIN_ROWS = NA_Q_ROWS + NA_KH
VMEM_LIMIT = 56 * 1024 * 1024


def _cparams(sem, flags=None):
    return pltpu.CompilerParams(dimension_semantics=sem, vmem_limit_bytes=VMEM_LIMIT, flags=flags)


def _dot(a, b):
    return jnp.dot(a, b, preferred_element_type=F32)


def _dot_exact(a, b):
    return jnp.dot(a, b, preferred_element_type=F32, precision=lax.Precision.HIGHEST)


def _mod_kernel(c_ref, w_ref, b_ref, o_ref):
    c = c_ref[...]
    s = c / (1.0 + jnp.exp(-c))
    o_ref[...] = _dot_exact(s, w_ref[...]) + b_ref[...]


def _modulation(cs, w_mod, b_mod):
    depth, d, n = w_mod.shape
    nb = 1024
    return pl.pallas_call(
        _mod_kernel,
        grid=(depth, n // nb),
        in_specs=[pl.BlockSpec((8, d), lambda l, j: (0, 0)),
                  pl.BlockSpec((None, d, nb), lambda l, j: (l, 0, j)),
                  pl.BlockSpec((None, 1, nb), lambda l, j: (l, 0, j))],
        out_specs=pl.BlockSpec((None, 8, nb), lambda l, j: (l, 0, j)),
        out_shape=jax.ShapeDtypeStruct((depth, 8, n), F32),
        compiler_params=_cparams(("arbitrary", "arbitrary")),
        name="modulation",
    )(cs, w_mod, b_mod.reshape(depth, 1, n))


def _rms(x, axis, eps):
    return x * lax.rsqrt(jnp.mean(jnp.square(x), axis=axis, keepdims=True) + eps)


def _rope_lanes(x, lane, half, cos, sin):
    partner = jnp.where((lane & half) == 0, pltpu.roll(x, HEAD_PAD - half, 1), pltpu.roll(x, half, 1))
    return x * cos + partner * sin


def _rope_rows(x, half, cos, sin):
    partner = jnp.concatenate([x[half:2 * half], x[0:half], x[3 * half:4 * half], x[2 * half:3 * half]], axis=0)
    return x * cos + partner * sin


def _proj_kernel(x_ref, mod_ref, wtok_ref, wt_ref,
                 caq_ref, saq_ref, cak_ref, sak_ref, cmq_ref, smq_ref, cmk_ref, smk_ref,
                 gqt_ref, wuqt_ref, gkv_ref, wuk_ref, gkvt_ref, wuvt_ref,
                 qat_ref, ka_ref, vat_ref, qmt_ref, km_ref, vmt_ref, nq_ref, nkt_ref, nv_ref):
    x = x_ref[...]
    h = (x * (1.0 + mod_ref[1:2, :]) + mod_ref[0:1, :]).astype(BF16)
    p = _dot(h, wtok_ref[...])
    pt = lax.dot_general(wt_ref[...], h, (((1,), (1,)), ((), ())),
                         preferred_element_type=F32)
    tm = x.shape[0]
    lane = lax.broadcasted_iota(jnp.int32, (tm, HEAD_PAD), 1)

    caq, saq = caq_ref[...], saq_ref[...]
    for g in range(2 * N_HEADS):
        rows = slice(g * DA_DIM, (g + 1) * DA_DIM)
        qat_ref[rows, :] = _rope_rows(pt[rows, :], DA_DIM // 4, caq, saq).astype(BF16)
    cak, sak = cak_ref[...], sak_ref[...]
    for g in range(N_HEADS):
        cols = slice(g * HEAD_PAD, (g + 1) * HEAD_PAD)
        ka_ref[:, cols] = _rope_lanes(p[:, cols], lane, DA_DIM // 4, cak, sak).astype(BF16)
    ones = jnp.ones((ONES_ROWS, tm), BF16)
    for g in range(N_HEADS):
        r0 = g * DA_VROWS
        vat_ref[r0:r0 + DA_VDIM, :] = pt[512 + g * DA_VDIM:512 + (g + 1) * DA_VDIM, :].astype(BF16)
        vat_ref[r0 + DA_VDIM:r0 + DA_VROWS, :] = ones

    cqnt = (_rms(pt[1024:1280, :], 0, RMS_EPS) * gqt_ref[...]).astype(BF16)
    qmt = _dot(wuqt_ref[...], cqnt)
    cmq, smq = cmq_ref[...], smq_ref[...]
    scale_m = (MLA_NOPE + MLA_ROPE) ** -0.5 * LOG2E
    for g in range(N_HEADS):
        r0 = g * HEAD_PAD
        qmt_ref[r0:r0 + MLA_NOPE, :] = (qmt[r0:r0 + MLA_NOPE, :] * scale_m).astype(BF16)
        qmt_ref[r0 + MLA_NOPE:r0 + MLA_NOPE + MLA_ROPE, :] = _rope_rows(
            qmt[r0 + MLA_NOPE:r0 + MLA_NOPE + MLA_ROPE, :], MLA_ROPE // 4, cmq, smq).astype(BF16)
        qmt_ref[r0 + MLA_NOPE + MLA_ROPE:r0 + HEAD_PAD, :] = jnp.zeros((MLA_ZPAD, tm), BF16)
    ckvn = (_rms(p[:, 1024:1152], -1, RMS_EPS) * gkv_ref[...]).astype(BF16)
    km = _dot(ckvn, wuk_ref[...]) + p[:, 512:1024]
    cmk, smk = cmk_ref[...], smk_ref[...]
    for g in range(N_HEADS):
        cols = slice(g * HEAD_PAD, (g + 1) * HEAD_PAD)
        km_ref[:, cols] = _rope_lanes(km[:, cols], lane, MLA_ROPE // 4, cmk, smk).astype(BF16)
    ckvnt = (_rms(pt[1280:1408, :], 0, RMS_EPS) * gkvt_ref[...]).astype(BF16)
    vmt = _dot(wuvt_ref[...], ckvnt)
    for g in range(N_HEADS):
        r0 = g * MLA_VROWS
        vmt_ref[r0:r0 + MLA_VDIM, :] = vmt[g * MLA_VDIM:(g + 1) * MLA_VDIM, :].astype(BF16)
        vmt_ref[r0 + MLA_VDIM:r0 + MLA_VROWS, :] = ones

    nq_ref[...] = (p[:, 1152:1408] * (NA_DIM ** -0.5 * LOG2E)).astype(BF16)
    nv_ref[...] = p[:, 1408:1664].astype(BF16)
    nkt_ref[...] = pt[1408:1664, :].astype(BF16)


def _projection(xall, mod_l, pw, tabs, n_ctx_row):
    b, t, d = xall.shape
    tm = TOK_TILE
    nt = t // tm

    def tok(width):
        return pl.BlockSpec((None, tm, width), lambda j, i: (i, j, 0))

    def trn(height):
        return pl.BlockSpec((None, height, tm), lambda j, i: (i, 0, j))

    def const(arr):
        return pl.BlockSpec(arr.shape, lambda j, i: (0,) * arr.ndim)

    def tab_tok():
        return pl.BlockSpec((tm, HEAD_PAD), lambda j, i: (j, 0))

    def tab_trn(height):
        return pl.BlockSpec((height, tm), lambda j, i: (0, j))

    consts = [pw["gqt"], pw["wuqt"], pw["gkv"], pw["wuk"], pw["gkvt"], pw["wuvt"]]
    in_specs = [tok(d),
                pl.BlockSpec((None, 6, d), lambda j, i: (jnp.where(j == nt - 1, n_ctx_row, i), 0, 0)),
                const(pw["wtok"]), const(pw["wt"]),
                tab_trn(64), tab_trn(64), tab_tok(), tab_tok(),
                tab_trn(32), tab_trn(32), tab_tok(), tab_tok(),
                ] + [const(a) for a in consts]
    va_rows, vm_rows = N_HEADS * DA_VROWS, N_HEADS * MLA_VROWS
    out_specs = [trn(512), tok(512), trn(va_rows), trn(512), tok(512), trn(vm_rows), tok(256), trn(256), tok(256)]
    out_shape = [jax.ShapeDtypeStruct((b, 512, t), BF16), jax.ShapeDtypeStruct((b, t, 512), BF16),
                 jax.ShapeDtypeStruct((b, va_rows, t), BF16), jax.ShapeDtypeStruct((b, 512, t), BF16),
                 jax.ShapeDtypeStruct((b, t, 512), BF16), jax.ShapeDtypeStruct((b, vm_rows, t), BF16),
                 jax.ShapeDtypeStruct((b, t, 256), BF16), jax.ShapeDtypeStruct((b, 256, t), BF16),
                 jax.ShapeDtypeStruct((b, t, 256), BF16)]
    return pl.pallas_call(
        _proj_kernel,
        grid=(nt, b),
        in_specs=in_specs,
        out_specs=out_specs,
        out_shape=out_shape,
        compiler_params=_cparams(("parallel", "parallel")),
        name="projection",
    )(xall, mod_l, pw["wtok"], pw["wt"],
      tabs["caq"], tabs["saq"], tabs["cak"], tabs["sak"], tabs["cmq"], tabs["smq"], tabs["cmk"], tabs["smk"],
      *consts)


def _attn_kernel(*refs, mode, n_k, tk, lam_init, aliased_out):
    o_ref, m_ref, acc_ref, s_ref, cm_ref, p_ref, al_ref = refs[-7:]
    refs = refs[:-8] if aliased_out else refs[:-7]
    if mode == "da":
        qt_ref, k_ref, vt_ref, lam_ref, g_ref = refs
        qt = qt_ref[...]
        zero = jnp.zeros((DA_DIM, qt.shape[1]), qt.dtype)
        qts = (jnp.concatenate([qt[:DA_DIM], zero], axis=0), jnp.concatenate([zero, qt[DA_DIM:]], axis=0))
        kcols = ((0, 128), (0, 128))
        vdim, vrows = DA_VDIM, ((0, DA_VROWS), (0, DA_VROWS))
    else:
        qt_ref, k_ref, vt_ref = refs
        qts = (qt_ref[0:128, :], qt_ref[128:256, :])
        kcols = ((0, 128), (128, 256))
        vdim, vrows = MLA_VDIM, ((0, MLA_VROWS), (MLA_VROWS, 2 * MLA_VROWS))

    m_ref[...] = jnp.full(m_ref.shape, NEG_BIG, F32)
    acc_ref[...] = jnp.zeros(acc_ref.shape, F32)

    def fold_rows(op, x, rows):
        n = x.shape[0]
        if n > ROW_BLOCK:
            acc = x[0:ROW_BLOCK]
            for g in range(1, n // ROW_BLOCK):
                acc = op(acc, x[g * ROW_BLOCK:(g + 1) * ROW_BLOCK])
            x, n = acc, ROW_BLOCK
        while n > rows:
            n //= 2
            x = op(x[:n], x[n:])
        return x

    def step(a, b, c):
        sizes = [x[2] for x in (a, b, c) if x is not None]
        for i in range(2):
            if b is not None:
                m_prev = m_ref[i]
                m_new = jnp.maximum(m_prev, cm_ref[b[0], i])
            cmax, pv = None, None
            for r in range(0, max(sizes), FINE):
                if a is not None and r < a[2]:
                    slot, start, _ = a
                    st = _dot(k_ref[pl.ds(start + r, FINE), kcols[i][0]:kcols[i][1]], qts[i])
                    s_ref[slot, i, r:r + FINE, :] = st
                    mx = fold_rows(jnp.maximum, st, 8)
                    cmax = mx if cmax is None else jnp.maximum(cmax, mx)
                if b is not None and r < b[2]:
                    for rr in range(r, r + FINE, ROW_BLOCK):
                        x = s_ref[b[0], i, rr:rr + ROW_BLOCK, :] - m_new
                        p_ref[b[0], i, rr:rr + ROW_BLOCK, :] = jnp.exp2(x.astype(BF16))
                if c is not None and r < c[2] and r % 256 == 0:
                    slot, start, _ = c
                    d = _dot(vt_ref[vrows[i][0]:vrows[i][1], pl.ds(start + r, 256)], p_ref[slot, i, r:r + 256, :])
                    pv = d if pv is None else pv + d
            if b is not None:
                al_ref[b[0], i] = jnp.exp2(m_prev - m_new)
                m_ref[i] = m_new
            if c is not None:
                acc_ref[i] = al_ref[c[0], i] * acc_ref[i] + pv
            if a is not None:
                cm_ref[a[0], i] = jnp.max(cmax, axis=0, keepdims=True)

    n_main, tail = n_k // tk, n_k % tk
    n_chunks = n_main + (1 if tail else 0)

    def static_chunk(t):
        if not 0 <= t < n_chunks:
            return None
        return (t % 2, t * tk, tk if t < n_main else tail)

    def static_step(t):
        step(static_chunk(t), static_chunk(t - 1), static_chunk(t - 2))

    lo, hi = 2, n_main - 1
    if hi >= lo:
        lo += (hi - lo + 1) % BODY_STAGES
    for t in range(0, min(lo, n_chunks + 2)):
        static_step(t)

    def body(j, carry):
        for u in range(BODY_STAGES):
            t = lo + BODY_STAGES * j + u
            par = (lo + u) % 2
            step((par, pl.multiple_of(t * tk, tk), tk), (1 - par, pl.multiple_of((t - 1) * tk, tk), tk),
                 (par, pl.multiple_of((t - 2) * tk, tk), tk))
        return carry

    if hi >= lo:
        lax.fori_loop(0, (hi - lo + 1) // BODY_STAGES, body, 0)
    for t in range(max(lo, hi + 1), n_chunks + 2):
        static_step(t)

    o0 = acc_ref[0, 0:vdim, :] / acc_ref[0, vdim:vdim + 1, :]
    o1 = acc_ref[1, 0:vdim, :] / acc_ref[1, vdim:vdim + 1, :]
    if mode == "da":
        lv = lam_ref[...]
        lam = (jnp.exp(jnp.sum(lv[0:1] * lv[1:2], axis=-1, keepdims=True))
               - jnp.exp(jnp.sum(lv[2:3] * lv[3:4], axis=-1, keepdims=True)) + lam_init)
        o = o0 - lam * o1
        o = _rms(o, 0, SUBLN_EPS) * g_ref[...] * (1.0 - lam_init)
    else:
        o = jnp.concatenate([o0, o1], axis=0)
    o_ref[...] = o.T.astype(o_ref.dtype)


def _dense_attention(mode, qt, k, vt, *, tq, tk, q_blk0, k_blk0, n_k, out, extra=(), lam_init=0.0):
    b, _, t = qt.shape
    groups = N_HEADS if mode == "da" else N_HEADS // 2
    qw = 128 if mode == "da" else 256
    vblk = DA_VROWS if mode == "da" else 2 * MLA_VROWS
    acc_rows = DA_VROWS if mode == "da" else MLA_VROWS
    aliased = not isinstance(out, int)
    if aliased:
        n_q, out_shape = tq, jax.ShapeDtypeStruct(out.shape, out.dtype)
    else:
        n_q, out_shape = out, jax.ShapeDtypeStruct((b, t, 128 * groups), BF16)
    in_specs = [pl.BlockSpec((None, qw, tq), lambda i, g, j: (i, g, q_blk0 + j)),
                pl.BlockSpec((None, n_k, qw), lambda i, g, j: (i, k_blk0, g)),
                pl.BlockSpec((None, vblk, n_k), lambda i, g, j: (i, g, k_blk0))]
    args = [qt, k, vt]
    for a in extra:
        in_specs.append(pl.BlockSpec(a.shape, lambda i, g, j: (0, 0)))
        args.append(a)
    alias = {}
    if aliased:
        in_specs.append(pl.BlockSpec(memory_space=pl.ANY))
        alias = {len(args): 0}
        args.append(out)
    return pl.pallas_call(
        functools.partial(_attn_kernel, mode=mode, n_k=n_k, tk=tk, lam_init=lam_init, aliased_out=aliased),
        grid=(b, groups, n_q // tq),
        in_specs=in_specs,
        out_specs=pl.BlockSpec((None, tq, 128), lambda i, g, j: (i, q_blk0 + j, g)),
        out_shape=out_shape,
        scratch_shapes=[pltpu.VMEM((2, 1, tq), F32),
                        pltpu.VMEM((2, acc_rows, tq), F32), pltpu.VMEM((2, 2, tk, tq), F32),
                        pltpu.VMEM((2, 2, 1, tq), F32), pltpu.VMEM((2, 2, tk, tq), BF16),
                        pltpu.VMEM((2, 2, 1, tq), F32)],
        input_output_aliases=alias,
        compiler_params=_cparams(("parallel", "parallel", "parallel")),
        name="attn_" + mode,
    )(*args)


def _na_kernel(*refs, rows, ctx_off, n_ctx, use_window):
    if use_window:
        q_ref, kt_ref, v_ref, bias_ref, o_ref = refs
    else:
        q_ref, kt_ref, v_ref, o_ref = refs
    q = q_ref[...]
    lane = lax.broadcasted_iota(jnp.int32, q.shape, 1)
    zero = jnp.zeros_like(q)
    kt_c = kt_ref[:, ctx_off:ctx_off + n_ctx]
    v_c = v_ref[ctx_off:ctx_off + n_ctx, :]
    if use_window:
        i = pl.program_id(1)
        kr0 = jnp.clip(i * NA_Q_ROWS - NA_KH // 2, 0, rows - NA_WIN_ROWS)
        start = pl.multiple_of(kr0 * GRID_W, 128)
        kt_w = kt_ref[:, pl.ds(start, NA_WIN_ROWS * GRID_W)]
        v_w = v_ref[pl.ds(start, NA_WIN_ROWS * GRID_W), :]
    out = jnp.zeros(q.shape, F32)
    for h in range(N_HEADS):
        hm = (lane >= h * NA_DIM) & (lane < (h + 1) * NA_DIM)
        qh = jnp.where(hm, q, zero)
        s_c = _dot(qh, kt_c)
        m = jnp.max(s_c, axis=-1, keepdims=True)
        if use_window:
            s_w = _dot(qh, kt_w) + bias_ref[h]
            m = jnp.maximum(m, jnp.max(s_w, axis=-1, keepdims=True))
            p_w = jnp.exp2(s_w - m)
        p_c = jnp.exp2(s_c - m)
        l = jnp.sum(p_c, axis=-1, keepdims=True)
        o = _dot(p_c.astype(BF16), v_c)
        if use_window:
            l = l + jnp.sum(p_w, axis=-1, keepdims=True)
            o = o + _dot(p_w.astype(BF16), v_w)
        out = jnp.where(hm, o / l, out)
    o_ref[...] = out.astype(o_ref.dtype)


def _na_bias_index(rows):
    nb = rows // NA_Q_ROWS
    assert rows % NA_Q_ROWS == 0 and nb >= 3 and rows >= NA_WIN_ROWS and (rows - NA_WIN_ROWS) % 2 == 0

    def block(i):
        r = i * NA_Q_ROWS + np.arange(NA_Q_ROWS)
        kr = int(np.clip(i * NA_Q_ROWS - NA_KH // 2, 0, rows - NA_WIN_ROWS)) + np.arange(NA_WIN_ROWS)
        rs = np.clip(r - NA_KH // 2, 0, rows - NA_KH)
        valid = (kr[None, :] >= rs[:, None]) & (kr[None, :] < rs[:, None] + NA_KH)
        dy = np.clip(kr[None, :] - r[:, None] + NA_KH - 1, 0, 2 * NA_KH - 2)
        return valid, dy

    blocks = [block(i) for i in range(nb)]
    for i in range(2, nb - 1):
        assert all(np.array_equal(a, b_) for a, b_ in zip(blocks[1], blocks[i]))
    cls = [blocks[0], blocks[1], blocks[nb - 1]]
    rvalid = np.stack([c[0] for c in cls]).reshape(-1)
    dy = np.stack([c[1] for c in cls]).reshape(-1)
    rsel = (dy[:, None] == np.arange(2 * NA_KH - 1)[None, :]).astype(np.float32)

    c = np.arange(GRID_W)
    cs = np.clip(c - NA_KW // 2, 0, GRID_W - NA_KW)
    cvalid = ((c[None, :] >= cs[:, None]) & (c[None, :] < cs[:, None] + NA_KW)).reshape(-1)
    dx = np.clip(c[None, :] - c[:, None] + NA_KW - 1, 0, 2 * NA_KW - 2).reshape(-1)
    csel = (np.arange(2 * NA_KW - 1)[:, None] == dx[None, :]).astype(np.float32)
    return rsel, rvalid.astype(np.float32)[:, None], csel, cvalid.astype(np.float32)[None, :]


def _na_bias_kernel(rpb_ref, rsel_ref, rmask_ref, csel_ref, cmask_ref, o_ref):
    cols = _dot_exact(rpb_ref[...], csel_ref[...])
    vals = _dot_exact(rsel_ref[...], cols)
    valid = (rmask_ref[...] * cmask_ref[...]) > 0.5
    o_ref[...] = jnp.where(valid, vals * LOG2E, NEG_BIG)


def _na_bias(rpb, rows):
    rsel, rmask, csel, cmask = (jnp.asarray(a) for a in _na_bias_index(rows))
    nh = rpb.shape[0]
    n_r, n_c = rsel.shape[0], csel.shape[1]
    full = lambda a: pl.BlockSpec(a.shape, lambda h: (0, 0))
    out = pl.pallas_call(
        _na_bias_kernel,
        grid=(nh,),
        in_specs=[pl.BlockSpec((None,) + rpb.shape[1:], lambda h: (h, 0, 0)),
                  full(rsel), full(rmask), full(csel), full(cmask)],
        out_specs=pl.BlockSpec((None, n_r, n_c), lambda h: (h, 0, 0)),
        out_shape=jax.ShapeDtypeStruct((nh, n_r, n_c), F32),
        compiler_params=_cparams(("parallel",)),
        name="na_bias",
    )(rpb.astype(F32), rsel, rmask, csel, cmask)
    out = out.reshape(nh, 3, NA_Q_ROWS, NA_WIN_ROWS, GRID_W, GRID_W)
    out = jnp.transpose(out, (1, 0, 2, 4, 3, 5))
    return out.reshape(3, nh, NA_Q_ROWS * GRID_W, NA_WIN_ROWS * GRID_W)


def _neighbourhood_attention(nq, nkt, nv, bias, seq):
    b, t, w = nq.shape
    tq = NA_Q_ROWS * GRID_W
    nb = seq // tq
    rows = seq // GRID_W
    return pl.pallas_call(
        functools.partial(_na_kernel, rows=rows, ctx_off=seq, n_ctx=t - seq, use_window=True),
        grid=(b, nb),
        in_specs=[pl.BlockSpec((None, tq, w), lambda i, j: (i, j, 0)),
                  pl.BlockSpec((None, w, t), lambda i, j: (i, 0, 0)),
                  pl.BlockSpec((None, t, w), lambda i, j: (i, 0, 0)),
                  pl.BlockSpec((None,) + bias.shape[1:],
                               lambda i, j: (jnp.where(j == 0, 0, jnp.where(j == nb - 1, 2, 1)), 0, 0, 0))],
        out_specs=pl.BlockSpec((None, tq, w), lambda i, j: (i, j, 0)),
        out_shape=jax.ShapeDtypeStruct((b, t, w), BF16),
        compiler_params=_cparams(("parallel", "parallel")),
        name="attn_na",
    )(nq, nkt, nv, bias)


def _neighbourhood_attention_ctx(nq, nkt, nv, out, seq):
    b, t, w = nq.shape
    n_ctx = t - seq
    blk = seq // n_ctx
    kern = lambda q, k, v, _, o, **kw: _na_kernel(q, k, v, o, **kw)
    return pl.pallas_call(
        functools.partial(kern, rows=0, ctx_off=0, n_ctx=n_ctx, use_window=False),
        grid=(b,),
        in_specs=[pl.BlockSpec((None, n_ctx, w), lambda i: (i, blk, 0)),
                  pl.BlockSpec((None, w, n_ctx), lambda i: (i, 0, blk)),
                  pl.BlockSpec((None, n_ctx, w), lambda i: (i, blk, 0)),
                  pl.BlockSpec(memory_space=pl.ANY)],
        out_specs=pl.BlockSpec((None, n_ctx, w), lambda i: (i, blk, 0)),
        out_shape=jax.ShapeDtypeStruct(out.shape, out.dtype),
        input_output_aliases={3: 0},
        compiler_params=_cparams(("parallel",)),
        name="attn_na_ctx",
    )(nq, nkt, nv, out)


def _layer_norm(z, g, b):
    mu = jnp.mean(z, axis=-1, keepdims=True)
    zc = z - mu
    var = jnp.mean(jnp.square(zc), axis=-1, keepdims=True)
    return zc * lax.rsqrt(var + LN_EPS) * g + b


def _ffn_kernel(x_ref, ya_ref, yb_ref, yc_ref, mod_ref, wo_ref, w1_ref, w2_ref, ln_ref, o_ref, *, alpha):
    x = x_ref[...]
    a = (_dot(ya_ref[...], wo_ref[0:512, :]) + _dot(yb_ref[...], wo_ref[512:768, :])
         + _dot(yc_ref[...], wo_ref[768:1024, :]))
    x1 = _layer_norm(alpha * x + mod_ref[2:3, :] * a, ln_ref[0:1, :], ln_ref[1:2, :])
    h2 = (x1 * (1.0 + mod_ref[4:5, :]) + mod_ref[3:4, :]).astype(BF16)
    u = jnp.maximum(_dot(h2, w1_ref[...]), 0.0)
    f = _dot(jnp.square(u).astype(BF16), w2_ref[...])
    o_ref[...] = _layer_norm(alpha * x1 + mod_ref[5:6, :] * f, ln_ref[2:3, :], ln_ref[3:4, :])


def _out_ffn(xall, ya, yb, yc, mod_l, wo, w1, w2, ln, n_tok, n_ctx_row, alpha):
    b, t, d = xall.shape
    tm = TOK_TILE
    nt_all = t // tm
    nt = n_tok // tm

    def tok(width):
        return pl.BlockSpec((None, tm, width), lambda i, j: (i, j, 0))

    def const(arr):
        return pl.BlockSpec(arr.shape, lambda i, j: (0,) * arr.ndim, pipeline_mode=pl.Buffered(1))

    return pl.pallas_call(
        functools.partial(_ffn_kernel, alpha=alpha),
        grid=(b, nt),
        in_specs=[tok(d), tok(512), tok(256), tok(256),
                  pl.BlockSpec((None, 6, d), lambda i, j: (jnp.where(j == nt_all - 1, n_ctx_row, i), 0, 0)),
                  const(wo), const(w1), const(w2), const(ln)],
        out_specs=tok(d),
        out_shape=jax.ShapeDtypeStruct((b, n_tok, d), F32),
        compiler_params=_cparams(("parallel", "parallel")),
        name="out_ffn",
    )(xall, ya, yb, yc, mod_l, wo, w1, w2, ln)


def _rope_tables(seq, n_ctx):
    t = jnp.arange(seq, dtype=jnp.int32)
    row = (t // GRID_W).astype(F32)
    col = (t % GRID_W).astype(F32)

    def axial(n):
        half = n // 2
        inv = ROPE_BASE ** (-2.0 * jnp.arange(half, dtype=F32) / n)
        ar, ac = row[:, None] * inv[None, :], col[:, None] * inv[None, :]
        cos = jnp.concatenate([jnp.cos(ar), jnp.cos(ar), jnp.cos(ac), jnp.cos(ac)], axis=-1)
        sin = jnp.concatenate([-jnp.sin(ar), jnp.sin(ar), -jnp.sin(ac), jnp.sin(ac)], axis=-1)
        cos = jnp.concatenate([cos, jnp.ones((n_ctx, 2 * n), F32)], axis=0)
        sin = jnp.concatenate([sin, jnp.zeros((n_ctx, 2 * n), F32)], axis=0)
        return cos, sin

    cos_a, sin_a = axial(DA_DIM // 2)
    cos_m, sin_m = axial(MLA_ROPE // 2)
    sa = DA_DIM ** -0.5 * LOG2E
    sm = (MLA_NOPE + MLA_ROPE) ** -0.5 * LOG2E
    tt = seq + n_ctx
    ones = jnp.ones((tt, MLA_NOPE), F32)
    zpad = jnp.zeros((tt, MLA_ZPAD), F32)
    return dict(
        caq=cos_a.T * sa, saq=sin_a.T * sa,
        cak=jnp.tile(cos_a, (1, 2)), sak=jnp.tile(sin_a, (1, 2)),
        cmq=cos_m.T * sm, smq=sin_m.T * sm,
        cmk=jnp.concatenate([ones, cos_m, zpad], axis=-1),
        smk=jnp.concatenate([0.0 * ones, sin_m, zpad], axis=-1))


def _layer_weights(w_in, gq, w_uq, gkv, w_ukv):
    d = w_in.shape[0]
    o = np.cumsum([0, 512, 512, 512, MLA_Q_RANK, MLA_KV_RANK, MLA_ROPE, 256, 256, 256])
    col = lambda k: w_in[:, int(o[k]):int(o[k + 1])]
    aq, ak, av, cq, ckv, kr, nq, nk, nv = (col(k) for k in range(9))
    kr_placed = jnp.concatenate([jnp.zeros((d, MLA_NOPE), w_in.dtype), kr,
                                 jnp.zeros((d, MLA_ZPAD), w_in.dtype)], axis=1)
    wtok = jnp.concatenate([ak] + [kr_placed] * N_HEADS + [ckv, nq, nv], axis=1).astype(BF16)
    wt = jnp.concatenate([aq, av, cq, ckv, nk], axis=1).T.astype(BF16)
    uq = w_uq.reshape(MLA_Q_RANK, N_HEADS, MLA_NOPE + MLA_ROPE)
    uq = jnp.pad(uq, ((0, 0), (0, 0), (0, MLA_ZPAD)))
    ukv = w_ukv.reshape(MLA_KV_RANK, N_HEADS, MLA_NOPE + MLA_VDIM)
    uk = jnp.pad(ukv[:, :, :MLA_NOPE], ((0, 0), (0, 0), (0, HEAD_PAD - MLA_NOPE)))
    return dict(
        wtok=wtok, wt=wt,
        gqt=gq.reshape(-1, 1).astype(F32),
        wuqt=uq.reshape(MLA_Q_RANK, N_HEADS * HEAD_PAD).T.astype(BF16),
        gkv=gkv.reshape(1, -1).astype(F32), gkvt=gkv.reshape(-1, 1).astype(F32),
        wuk=uk.reshape(MLA_KV_RANK, N_HEADS * HEAD_PAD).astype(BF16),
        wuvt=ukv[:, :, MLA_NOPE:].reshape(MLA_KV_RANK, N_HEADS * MLA_VDIM).T.astype(BF16))


def kernel(x, c, ctx, c_ctx, w_mod, b_mod, w_in, da_lam_q1, da_lam_k1, da_lam_q2, da_lam_k2, da_subln_g,
           mla_q_norm_g, mla_w_uq, mla_kv_norm_g, mla_w_ukv, na_rpb, w_out, ln1_g, ln1_b, w_ff1, w_ff2,
           ln2_g, ln2_b):
    b, seq, d = x.shape
    n_ctx = ctx.shape[1]
    depth = w_mod.shape[0]
    assert seq % TOK_TILE == 0 and n_ctx == TOK_TILE and b < 8 and seq % (NA_Q_ROWS * GRID_W) == 0
    alpha = (2.0 * depth) ** 0.25
    tq, tk = ATTN_TQ, ATTN_TK
    assert seq % tq == 0 and n_ctx % 128 == 0

    cs = jnp.zeros((8, d), F32).at[:b].set(c).at[b].set(c_ctx)
    mod = _modulation(cs, w_mod, b_mod).reshape(depth, 8, 6, d)
    tabs = _rope_tables(seq, n_ctx)
    xall = jnp.concatenate([x, ctx], axis=1)
    ctx_blk = seq // n_ctx

    for l in range(depth):
        last = l == depth - 1
        lam_init = 0.8 - 0.6 * math.exp(-0.3 * l)
        pw = _layer_weights(w_in[l], mla_q_norm_g[l], mla_w_uq[l], mla_kv_norm_g[l], mla_w_ukv[l])
        qat, ka, vat, qmt, km, vmt, nq, nkt, nv = _projection(xall, mod[l], pw, tabs, b)

        lam_vec = jnp.stack([da_lam_q1[l], da_lam_k1[l], da_lam_q2[l], da_lam_k2[l]]).astype(F32)
        g_sub = da_subln_g[l].reshape(-1, 1).astype(F32)
        ya = _dense_attention("da", qat, ka, vat, tq=tq, tk=tk, q_blk0=0, k_blk0=0, n_k=seq + n_ctx,
                              out=seq, extra=(lam_vec, g_sub), lam_init=lam_init)
        yb = _dense_attention("mla", qmt, km, vmt, tq=tq, tk=tk, q_blk0=0, k_blk0=0, n_k=seq + n_ctx, out=seq)
        yc = _neighbourhood_attention(nq, nkt, nv, _na_bias(na_rpb[l], seq // GRID_W), seq)
        if not last:
            ya = _dense_attention("da", qat, ka, vat, tq=n_ctx, tk=n_ctx, q_blk0=ctx_blk, k_blk0=ctx_blk,
                                  n_k=n_ctx, out=ya, extra=(lam_vec, g_sub), lam_init=lam_init)
            yb = _dense_attention("mla", qmt, km, vmt, tq=n_ctx, tk=n_ctx, q_blk0=ctx_blk, k_blk0=ctx_blk,
                                  n_k=n_ctx, out=yb)
            yc = _neighbourhood_attention_ctx(nq, nkt, nv, yc, seq)

        ln = jnp.stack([ln1_g[l], ln1_b[l], ln2_g[l], ln2_b[l]]).astype(F32)
        xall = _out_ffn(xall, ya, yb, yc, mod[l], w_out[l].astype(BF16), w_ff1[l].astype(BF16),
                        w_ff2[l].astype(BF16), ln, seq if last else seq + n_ctx, b, alpha)
    return xall
```

```python
import functools
import math

import numpy as np
import jax
import jax.numpy as jnp
from jax import lax
from jax.experimental import pallas as pl
from jax.experimental.pallas import tpu as pltpu

F32 = jnp.float32
BF16 = jnp.bfloat16

GRID_W = 64
ROPE_BASE = 10000.0
LN_EPS = 1e-6
RMS_EPS = 1e-6
SUBLN_EPS = 1e-5
N_HEADS = 4
DA_DIM = 64
DA_VDIM = 128
MLA_Q_RANK = 256
MLA_KV_RANK = 128
MLA_NOPE = 64
MLA_ROPE = 32
MLA_VDIM = 64
NA_DIM = 64
NA_KH = 8
NA_KW = 16
HEAD_PAD = 128
MLA_ZPAD = HEAD_PAD - MLA_NOPE - MLA_ROPE
ONES_ROWS = 16
DA_VROWS = DA_VDIM + ONES_ROWS
MLA_VROWS = MLA_VDIM + ONES_ROWS

LOG2E = math.log2(math.e)
NEG_BIG = -1e30

TOK_TILE = 256
ATTN_TQ = 256
ATTN_TK_MAX = 1664
FINE = 128
ROW_BLOCK = 64
NA_Q_ROWS = 4
NA_WIN_ROWS = NA_Q_ROWS + NA_KH
VMEM_LIMIT = 56 * 1024 * 1024


def _cparams(sem, flags=None):
    return pltpu.CompilerParams(dimension_semantics=sem, vmem_limit_bytes=VMEM_LIMIT, flags=flags)


def _dot(a, b):
    return jnp.dot(a, b, preferred_element_type=F32)


def _dot_exact(a, b):
    return jnp.dot(a, b, preferred_element_type=F32, precision=lax.Precision.HIGHEST)


def _mod_kernel(c_ref, w_ref, b_ref, o_ref):
    c = c_ref[...]
    s = c / (1.0 + jnp.exp(-c))
    o_ref[...] = _dot_exact(s, w_ref[...]) + b_ref[...]


def _modulation(cs, w_mod, b_mod):
    depth, d, n = w_mod.shape
    nb = 1024
    return pl.pallas_call(
        _mod_kernel,
        grid=(depth, n // nb),
        in_specs=[pl.BlockSpec((8, d), lambda l, j: (0, 0)),
                  pl.BlockSpec((None, d, nb), lambda l, j: (l, 0, j)),
                  pl.BlockSpec((None, 1, nb), lambda l, j: (l, 0, j))],
        out_specs=pl.BlockSpec((None, 8, nb), lambda l, j: (l, 0, j)),
        out_shape=jax.ShapeDtypeStruct((depth, 8, n), F32),
        compiler_params=_cparams(("arbitrary", "arbitrary")),
        name="modulation",
    )(cs, w_mod, b_mod.reshape(depth, 1, n))


def _rms(x, axis, eps):
    return x * lax.rsqrt(jnp.mean(jnp.square(x), axis=axis, keepdims=True) + eps)


def _rope_lanes(x, lane, half, cos, sin):
    partner = jnp.where((lane & half) == 0, pltpu.roll(x, HEAD_PAD - half, 1), pltpu.roll(x, half, 1))
    return x * cos + partner * sin


def _rope_rows(x, half, cos, sin):
    partner = jnp.concatenate([x[half:2 * half], x[0:half], x[3 * half:4 * half], x[2 * half:3 * half]], axis=0)
    return x * cos + partner * sin


def _proj_kernel(x_ref, mod_ref, wtok_ref, wt_ref,
                 caq_ref, saq_ref, cak_ref, sak_ref, cmq_ref, smq_ref, cmk_ref, smk_ref,
                 gqt_ref, wuqt_ref, gkv_ref, wuk_ref, gkvt_ref, wuvt_ref,
                 qat_ref, ka_ref, vat_ref, qmt_ref, km_ref, vmt_ref, nq_ref, nkt_ref, nv_ref):
    x = x_ref[...]
    h = (x * (1.0 + mod_ref[1:2, :]) + mod_ref[0:1, :]).astype(BF16)
    p = _dot(h, wtok_ref[...])
    pt = lax.dot_general(wt_ref[...], h, (((1,), (1,)), ((), ())),
                         preferred_element_type=F32)
    tm = x.shape[0]
    lane = lax.broadcasted_iota(jnp.int32, (tm, HEAD_PAD), 1)

    caq, saq = caq_ref[...], saq_ref[...]
    for g in range(2 * N_HEADS):
        rows = slice(g * DA_DIM, (g + 1) * DA_DIM)
        qat_ref[rows, :] = _rope_rows(pt[rows, :], DA_DIM // 4, caq, saq).astype(BF16)
    cak, sak = cak_ref[...], sak_ref[...]
    for g in range(N_HEADS):
        cols = slice(g * HEAD_PAD, (g + 1) * HEAD_PAD)
        ka_ref[:, cols] = _rope_lanes(p[:, cols], lane, DA_DIM // 4, cak, sak).astype(BF16)
    ones = jnp.ones((ONES_ROWS, tm), BF16)
    for g in range(N_HEADS):
        r0 = g * DA_VROWS
        vat_ref[r0:r0 + DA_VDIM, :] = pt[512 + g * DA_VDIM:512 + (g + 1) * DA_VDIM, :].astype(BF16)
        vat_ref[r0 + DA_VDIM:r0 + DA_VROWS, :] = ones

    cqnt = (_rms(pt[1024:1280, :], 0, RMS_EPS) * gqt_ref[...]).astype(BF16)
    qmt = _dot(wuqt_ref[...], cqnt)
    cmq, smq = cmq_ref[...], smq_ref[...]
    scale_m = (MLA_NOPE + MLA_ROPE) ** -0.5 * LOG2E
    for g in range(N_HEADS):
        r0 = g * HEAD_PAD
        qmt_ref[r0:r0 + MLA_NOPE, :] = (qmt[r0:r0 + MLA_NOPE, :] * scale_m).astype(BF16)
        qmt_ref[r0 + MLA_NOPE:r0 + MLA_NOPE + MLA_ROPE, :] = _rope_rows(
            qmt[r0 + MLA_NOPE:r0 + MLA_NOPE + MLA_ROPE, :], MLA_ROPE // 4, cmq, smq).astype(BF16)
        qmt_ref[r0 + MLA_NOPE + MLA_ROPE:r0 + HEAD_PAD, :] = jnp.zeros((MLA_ZPAD, tm), BF16)
    ckvn = (_rms(p[:, 1024:1152], -1, RMS_EPS) * gkv_ref[...]).astype(BF16)
    km = _dot(ckvn, wuk_ref[...]) + p[:, 512:1024]
    cmk, smk = cmk_ref[...], smk_ref[...]
    for g in range(N_HEADS):
        cols = slice(g * HEAD_PAD, (g + 1) * HEAD_PAD)
        km_ref[:, cols] = _rope_lanes(km[:, cols], lane, MLA_ROPE // 4, cmk, smk).astype(BF16)
    ckvnt = (_rms(pt[1280:1408, :], 0, RMS_EPS) * gkvt_ref[...]).astype(BF16)
    vmt = _dot(wuvt_ref[...], ckvnt)
    for g in range(N_HEADS):
        r0 = g * MLA_VROWS
        vmt_ref[r0:r0 + MLA_VDIM, :] = vmt[g * MLA_VDIM:(g + 1) * MLA_VDIM, :].astype(BF16)
        vmt_ref[r0 + MLA_VDIM:r0 + MLA_VROWS, :] = ones

    nq_ref[...] = (p[:, 1152:1408] * (NA_DIM ** -0.5 * LOG2E)).astype(BF16)
    nv_ref[...] = p[:, 1408:1664].astype(BF16)
    nkt_ref[...] = pt[1408:1664, :].astype(BF16)


def _projection(xall, mod_l, pw, tabs, n_ctx_row):
    b, t, d = xall.shape
    tm = TOK_TILE
    nt = t // tm

    def tok(width):
        return pl.BlockSpec((None, tm, width), lambda j, i: (i, j, 0))

    def trn(height):
        return pl.BlockSpec((None, height, tm), lambda j, i: (i, 0, j))

    def const(arr):
        return pl.BlockSpec(arr.shape, lambda j, i: (0,) * arr.ndim)

    def tab_tok():
        return pl.BlockSpec((tm, HEAD_PAD), lambda j, i: (j, 0))

    def tab_trn(height):
        return pl.BlockSpec((height, tm), lambda j, i: (0, j))

    consts = [pw["gqt"], pw["wuqt"], pw["gkv"], pw["wuk"], pw["gkvt"], pw["wuvt"]]
    in_specs = [tok(d),
                pl.BlockSpec((None, 6, d), lambda j, i: (jnp.where(j == nt - 1, n_ctx_row, i), 0, 0)),
                const(pw["wtok"]), const(pw["wt"]),
                tab_trn(64), tab_trn(64), tab_tok(), tab_tok(),
                tab_trn(32), tab_trn(32), tab_tok(), tab_tok(),
                ] + [const(a) for a in consts]
    va_rows, vm_rows = N_HEADS * DA_VROWS, N_HEADS * MLA_VROWS
    out_specs = [trn(512), tok(512), trn(va_rows), trn(512), tok(512), trn(vm_rows), tok(256), trn(256), tok(256)]
    out_shape = [jax.ShapeDtypeStruct((b, 512, t), BF16), jax.ShapeDtypeStruct((b, t, 512), BF16),
                 jax.ShapeDtypeStruct((b, va_rows, t), BF16), jax.ShapeDtypeStruct((b, 512, t), BF16),
                 jax.ShapeDtypeStruct((b, t, 512), BF16), jax.ShapeDtypeStruct((b, vm_rows, t), BF16),
                 jax.ShapeDtypeStruct((b, t, 256), BF16), jax.ShapeDtypeStruct((b, 256, t), BF16),
                 jax.ShapeDtypeStruct((b, t, 256), BF16)]
    return pl.pallas_call(
        _proj_kernel,
        grid=(nt, b),
        in_specs=in_specs,
        out_specs=out_specs,
        out_shape=out_shape,
        compiler_params=_cparams(("parallel", "parallel")),
        name="projection",
    )(xall, mod_l, pw["wtok"], pw["wt"],
      tabs["caq"], tabs["saq"], tabs["cak"], tabs["sak"], tabs["cmq"], tabs["smq"], tabs["cmk"], tabs["smk"],
      *consts)


def _attn_kernel(*refs, mode, tq, n_k, tk, lam_init, aliased_out):
    o_ref, m_ref, acc_ref, s_ref, cm_ref, p_ref, al_ref = refs[-7:]
    refs = refs[:-8] if aliased_out else refs[:-7]
    if mode == "da":
        qt_ref, k_ref, vt_ref, lam_ref, g_ref = refs
        kcols = ((0, 128), (0, 128))
        vdim, vrows = DA_VDIM, ((0, DA_VROWS), (0, DA_VROWS))
    else:
        qt_ref, k_ref, vt_ref = refs
        kcols = ((0, 128), (128, 256))
        vdim, vrows = MLA_VDIM, ((0, MLA_VROWS), (MLA_VROWS, 2 * MLA_VROWS))
    n_t, n_c = qt_ref.shape[1] // tq, n_k // tk
    total = n_t * n_c
    assert n_k % tk == 0 and tk % FINE == 0 and (total == 1 or n_c % 2 == 0)

    m_ref[...] = jnp.full(m_ref.shape, NEG_BIG, F32)
    acc_ref[...] = jnp.zeros(acc_ref.shape, F32)

    def query_maps(tile):
        col = tile * tq if isinstance(tile, int) else pl.multiple_of(tile * tq, tq)
        if mode == "da":
            qt = qt_ref[:, pl.ds(col, tq)]
            row = lax.broadcasted_iota(jnp.int32, qt.shape, 0)
            zero = jnp.zeros_like(qt)
            return (jnp.where(row < DA_DIM, qt, zero), jnp.where(row >= DA_DIM, qt, zero))
        return (qt_ref[0:128, pl.ds(col, tq)], qt_ref[128:256, pl.ds(col, tq)])

    def fold_rows(op, x, rows):
        n = x.shape[0]
        if n > ROW_BLOCK:
            acc = x[0:ROW_BLOCK]
            for g in range(1, n // ROW_BLOCK):
                acc = op(acc, x[g * ROW_BLOCK:(g + 1) * ROW_BLOCK])
            x, n = acc, ROW_BLOCK
        while n > rows:
            n //= 2
            x = op(x[:n], x[n:])
        return x

    def split(f):
        if isinstance(f, int):
            return f // n_c, (f % n_c) * tk
        tile = lax.div(f, jnp.int32(n_c))
        return tile, pl.multiple_of((f - tile * n_c) * tk, 128)

    def stage(f, par):
        static = isinstance(f, int)
        do_a = not static or f < total
        do_b = not static or 1 <= f <= total
        do_c = not static or 2 <= f <= total + 1
        if do_a:
            a_tile, a_start = split(f)
            qts = query_maps(a_tile)
        if do_b:
            _, b_start = split(f - 1)
        if do_c:
            _, c_start = split(f - 2)
        for i in range(2):
            if do_b:
                m_prev = jnp.where(b_start == 0, NEG_BIG, m_ref[i])
                m_new = jnp.maximum(m_prev, cm_ref[1 - par, i])
            cmax, pv = None, None
            for r in range(0, tk, FINE):
                if do_a:
                    st = _dot(k_ref[pl.ds(a_start + r, FINE), kcols[i][0]:kcols[i][1]], qts[i])
                    s_ref[par, i, r:r + FINE, :] = st
                    mx = fold_rows(jnp.maximum, st, 8)
                    cmax = mx if cmax is None else jnp.maximum(cmax, mx)
                if do_b:
                    for rr in range(r, r + FINE, ROW_BLOCK):
                        x = s_ref[1 - par, i, rr:rr + ROW_BLOCK, :] - m_new
                        p_ref[1 - par, i, rr:rr + ROW_BLOCK, :] = jnp.exp2(x.astype(BF16))
                if do_c and r % 256 == 0:
                    w = min(256, tk - r)
                    d = _dot(vt_ref[vrows[i][0]:vrows[i][1], pl.ds(c_start + r, w)], p_ref[par, i, r:r + w, :])
                    pv = d if pv is None else pv + d
            if do_b:
                al_ref[1 - par, i] = jnp.exp2(m_prev - m_new)
                m_ref[i] = m_new
            if do_c:
                acc_ref[i] = al_ref[par, i] * acc_ref[i] + pv
            if do_a:
                cm_ref[par, i] = jnp.max(cmax, axis=0, keepdims=True)

    def finish_tile(tile):
        o0 = acc_ref[0, 0:vdim, :] / acc_ref[0, vdim:vdim + 1, :]
        o1 = acc_ref[1, 0:vdim, :] / acc_ref[1, vdim:vdim + 1, :]
        if mode == "da":
            lv = lam_ref[...]
            lam = (jnp.exp(jnp.sum(lv[0:1] * lv[1:2], axis=-1, keepdims=True))
                   - jnp.exp(jnp.sum(lv[2:3] * lv[3:4], axis=-1, keepdims=True)) + lam_init)
            o = o0 - lam * o1
            o = _rms(o, 0, SUBLN_EPS) * g_ref[...] * (1.0 - lam_init)
        else:
            o = jnp.concatenate([o0, o1], axis=0)
        row = tile * tq if isinstance(tile, int) else pl.multiple_of(tile * tq, tq)
        o_ref[pl.ds(row, tq), :] = o.T.astype(o_ref.dtype)

    def static_stage(f):
        stage(f, f % 2)
        if f >= 2 and (f - 2) % n_c == n_c - 1:
            finish_tile((f - 2) // n_c)

    for f in range(0, min(2, total + 2)):
        static_stage(f)

    def body(j, carry):
        f0 = 2 * j
        stage(f0, 0)
        stage(f0 + 1, 1)

        @pl.when(lax.rem(f0 - 1, jnp.int32(n_c)) == n_c - 1)
        def _():
            finish_tile(lax.div(f0 - 1, jnp.int32(n_c)))
        return carry

    if total > 2:
        lax.fori_loop(1, total // 2, body, 0)
    for f in range(max(2, total), total + 2):
        static_stage(f)


def _dense_attention(mode, qt, k, vt, *, n_q, tq, tk, q_blk0, k_blk0, n_k, out=None, extra=(), lam_init=0.0):
    b, _, t = qt.shape
    groups = N_HEADS if mode == "da" else N_HEADS // 2
    qw = 128 if mode == "da" else 256
    vblk = DA_VROWS if mode == "da" else 2 * MLA_VROWS
    acc_rows = DA_VROWS if mode == "da" else MLA_VROWS
    aliased = out is not None
    out_shape = jax.ShapeDtypeStruct(out.shape if aliased else (b, t, 128 * groups), BF16)
    in_specs = [pl.BlockSpec((None, qw, n_q), lambda i, g: (i, g, q_blk0)),
                pl.BlockSpec((None, n_k, qw), lambda i, g: (i, k_blk0, g)),
                pl.BlockSpec((None, vblk, n_k), lambda i, g: (i, g, k_blk0))]
    args = [qt, k, vt]
    for a in extra:
        in_specs.append(pl.BlockSpec(a.shape, lambda i, g: (0, 0)))
        args.append(a)
    alias = {}
    if aliased:
        in_specs.append(pl.BlockSpec(memory_space=pl.ANY))
        alias = {len(args): 0}
        args.append(out)
    return pl.pallas_call(
        functools.partial(_attn_kernel, mode=mode, tq=tq, n_k=n_k, tk=tk, lam_init=lam_init,
                          aliased_out=aliased),
        grid=(b, groups),
        in_specs=in_specs,
        out_specs=pl.BlockSpec((None, n_q, 128), lambda i, g: (i, q_blk0, g)),
        out_shape=out_shape,
        scratch_shapes=[pltpu.VMEM((2, 1, tq), F32),
                        pltpu.VMEM((2, acc_rows, tq), F32), pltpu.VMEM((2, 2, tk, tq), F32),
                        pltpu.VMEM((2, 2, 1, tq), F32), pltpu.VMEM((2, 2, tk, tq), BF16),
                        pltpu.VMEM((2, 2, 1, tq), F32)],
        input_output_aliases=alias,
        compiler_params=_cparams(("parallel", "parallel")),
        name="attn_" + mode,
    )(*args)


def _na_kernel(*refs, rows, ctx_off, n_ctx, use_window):
    if use_window:
        q_ref, kt_ref, v_ref, bias_ref, o_ref = refs
    else:
        q_ref, kt_ref, v_ref, o_ref = refs
    q = q_ref[...]
    lane = lax.broadcasted_iota(jnp.int32, q.shape, 1)
    zero = jnp.zeros_like(q)
    kt_c = kt_ref[:, ctx_off:ctx_off + n_ctx]
    v_c = v_ref[ctx_off:ctx_off + n_ctx, :]
    if use_window:
        i = pl.program_id(1)
        kr0 = jnp.clip(i * NA_Q_ROWS - NA_KH // 2, 0, rows - NA_WIN_ROWS)
        start = pl.multiple_of(kr0 * GRID_W, 128)
        kt_w = kt_ref[:, pl.ds(start, NA_WIN_ROWS * GRID_W)]
        v_w = v_ref[pl.ds(start, NA_WIN_ROWS * GRID_W), :]
    out = jnp.zeros(q.shape, F32)
    for h in range(N_HEADS):
        hm = (lane >= h * NA_DIM) & (lane < (h + 1) * NA_DIM)
        qh = jnp.where(hm, q, zero)
        s_c = _dot(qh, kt_c)
        m = jnp.max(s_c, axis=-1, keepdims=True)
        if use_window:
            s_w = _dot(qh, kt_w) + bias_ref[h]
            m = jnp.maximum(m, jnp.max(s_w, axis=-1, keepdims=True))
            p_w = jnp.exp2(s_w - m)
        p_c = jnp.exp2(s_c - m)
        l = jnp.sum(p_c, axis=-1, keepdims=True)
        o = _dot(p_c.astype(BF16), v_c)
        if use_window:
            l = l + jnp.sum(p_w, axis=-1, keepdims=True)
            o = o + _dot(p_w.astype(BF16), v_w)
        out = jnp.where(hm, o / l, out)
    o_ref[...] = out.astype(o_ref.dtype)


def _na_bias_index(rows):
    nb = rows // NA_Q_ROWS
    assert rows % NA_Q_ROWS == 0 and nb >= 3 and rows >= NA_WIN_ROWS and (rows - NA_WIN_ROWS) % 2 == 0

    def block(i):
        r = i * NA_Q_ROWS + np.arange(NA_Q_ROWS)
        kr = int(np.clip(i * NA_Q_ROWS - NA_KH // 2, 0, rows - NA_WIN_ROWS)) + np.arange(NA_WIN_ROWS)
        rs = np.clip(r - NA_KH // 2, 0, rows - NA_KH)
        valid = (kr[None, :] >= rs[:, None]) & (kr[None, :] < rs[:, None] + NA_KH)
        dy = np.clip(kr[None, :] - r[:, None] + NA_KH - 1, 0, 2 * NA_KH - 2)
        return valid, dy

    blocks = [block(i) for i in range(nb)]
    for i in range(2, nb - 1):
        assert all(np.array_equal(a, b_) for a, b_ in zip(blocks[1], blocks[i]))
    cls = [blocks[0], blocks[1], blocks[nb - 1]]
    rvalid = np.stack([c[0] for c in cls]).reshape(-1)
    dy = np.stack([c[1] for c in cls]).reshape(-1)
    rsel = (dy[:, None] == np.arange(2 * NA_KH - 1)[None, :]).astype(np.float32)

    c = np.arange(GRID_W)
    cs = np.clip(c - NA_KW // 2, 0, GRID_W - NA_KW)
    cvalid = ((c[None, :] >= cs[:, None]) & (c[None, :] < cs[:, None] + NA_KW)).reshape(-1)
    dx = np.clip(c[None, :] - c[:, None] + NA_KW - 1, 0, 2 * NA_KW - 2).reshape(-1)
    csel = (np.arange(2 * NA_KW - 1)[:, None] == dx[None, :]).astype(np.float32)
    return rsel, rvalid.astype(np.float32)[:, None], csel, cvalid.astype(np.float32)[None, :]


def _na_bias_kernel(rpb_ref, rsel_ref, rmask_ref, csel_ref, cmask_ref, o_ref):
    cols = _dot_exact(rpb_ref[...], csel_ref[...])
    vals = _dot_exact(rsel_ref[...], cols)
    valid = (rmask_ref[...] * cmask_ref[...]) > 0.5
    o_ref[...] = jnp.where(valid, vals * LOG2E, NEG_BIG)


def _na_bias(rpb, rows):
    rsel, rmask, csel, cmask = (jnp.asarray(a) for a in _na_bias_index(rows))
    nh = rpb.shape[0]
    n_r, n_c = rsel.shape[0], csel.shape[1]
    full = lambda a: pl.BlockSpec(a.shape, lambda h: (0, 0))
    out = pl.pallas_call(
        _na_bias_kernel,
        grid=(nh,),
        in_specs=[pl.BlockSpec((None,) + rpb.shape[1:], lambda h: (h, 0, 0)),
                  full(rsel), full(rmask), full(csel), full(cmask)],
        out_specs=pl.BlockSpec((None, n_r, n_c), lambda h: (h, 0, 0)),
        out_shape=jax.ShapeDtypeStruct((nh, n_r, n_c), F32),
        compiler_params=_cparams(("parallel",)),
        name="na_bias",
    )(rpb.astype(F32), rsel, rmask, csel, cmask)
    out = out.reshape(nh, 3, NA_Q_ROWS, NA_WIN_ROWS, GRID_W, GRID_W)
    out = jnp.transpose(out, (1, 0, 2, 4, 3, 5))
    return out.reshape(3, nh, NA_Q_ROWS * GRID_W, NA_WIN_ROWS * GRID_W)


def _neighbourhood_attention(nq, nkt, nv, bias, seq):
    b, t, w = nq.shape
    tq = NA_Q_ROWS * GRID_W
    nb = seq // tq
    rows = seq // GRID_W
    return pl.pallas_call(
        functools.partial(_na_kernel, rows=rows, ctx_off=seq, n_ctx=t - seq, use_window=True),
        grid=(b, nb),
        in_specs=[pl.BlockSpec((None, tq, w), lambda i, j: (i, j, 0)),
                  pl.BlockSpec((None, w, t), lambda i, j: (i, 0, 0)),
                  pl.BlockSpec((None, t, w), lambda i, j: (i, 0, 0)),
                  pl.BlockSpec((None,) + bias.shape[1:],
                               lambda i, j: (jnp.where(j == 0, 0, jnp.where(j == nb - 1, 2, 1)), 0, 0, 0))],
        out_specs=pl.BlockSpec((None, tq, w), lambda i, j: (i, j, 0)),
        out_shape=jax.ShapeDtypeStruct((b, t, w), BF16),
        compiler_params=_cparams(("parallel", "parallel")),
        name="attn_na",
    )(nq, nkt, nv, bias)


def _neighbourhood_attention_ctx(nq, nkt, nv, out, seq):
    b, t, w = nq.shape
    n_ctx = t - seq
    blk = seq // n_ctx
    kern = lambda q, k, v, _, o, **kw: _na_kernel(q, k, v, o, **kw)
    return pl.pallas_call(
        functools.partial(kern, rows=0, ctx_off=0, n_ctx=n_ctx, use_window=False),
        grid=(b,),
        in_specs=[pl.BlockSpec((None, n_ctx, w), lambda i: (i, blk, 0)),
                  pl.BlockSpec((None, w, n_ctx), lambda i: (i, 0, blk)),
                  pl.BlockSpec((None, n_ctx, w), lambda i: (i, blk, 0)),
                  pl.BlockSpec(memory_space=pl.ANY)],
        out_specs=pl.BlockSpec((None, n_ctx, w), lambda i: (i, blk, 0)),
        out_shape=jax.ShapeDtypeStruct(out.shape, out.dtype),
        input_output_aliases={3: 0},
        compiler_params=_cparams(("parallel",)),
        name="attn_na_ctx",
    )(nq, nkt, nv, out)


def _layer_norm(z, g, b):
    mu = jnp.mean(z, axis=-1, keepdims=True)
    zc = z - mu
    var = jnp.mean(jnp.square(zc), axis=-1, keepdims=True)
    return zc * lax.rsqrt(var + LN_EPS) * g + b


def _ffn_kernel(x_ref, ya_ref, yb_ref, yc_ref, mod_ref, wo_ref, w1_ref, w2_ref, ln_ref, o_ref, *, alpha):
    x = x_ref[...]
    a = (_dot(ya_ref[...], wo_ref[0:512, :]) + _dot(yb_ref[...], wo_ref[512:768, :])
         + _dot(yc_ref[...], wo_ref[768:1024, :]))
    x1 = _layer_norm(alpha * x + mod_ref[2:3, :] * a, ln_ref[0:1, :], ln_ref[1:2, :])
    h2 = (x1 * (1.0 + mod_ref[4:5, :]) + mod_ref[3:4, :]).astype(BF16)
    u = jnp.maximum(_dot(h2, w1_ref[...]), 0.0)
    f = _dot(jnp.square(u).astype(BF16), w2_ref[...])
    o_ref[...] = _layer_norm(alpha * x1 + mod_ref[5:6, :] * f, ln_ref[2:3, :], ln_ref[3:4, :])


def _out_ffn(xall, ya, yb, yc, mod_l, wo, w1, w2, ln, n_tok, n_ctx_row, alpha):
    b, t, d = xall.shape
    tm = TOK_TILE
    nt_all = t // tm
    nt = n_tok // tm

    def tok(width):
        return pl.BlockSpec((None, tm, width), lambda i, j: (i, j, 0))

    def const(arr):
        return pl.BlockSpec(arr.shape, lambda i, j: (0,) * arr.ndim, pipeline_mode=pl.Buffered(1))

    return pl.pallas_call(
        functools.partial(_ffn_kernel, alpha=alpha),
        grid=(b, nt),
        in_specs=[tok(d), tok(512), tok(256), tok(256),
                  pl.BlockSpec((None, 6, d), lambda i, j: (jnp.where(j == nt_all - 1, n_ctx_row, i), 0, 0)),
                  const(wo), const(w1), const(w2), const(ln)],
        out_specs=tok(d),
        out_shape=jax.ShapeDtypeStruct((b, n_tok, d), F32),
        compiler_params=_cparams(("parallel", "parallel")),
        name="out_ffn",
    )(xall, ya, yb, yc, mod_l, wo, w1, w2, ln)


def _rope_tables(seq, n_ctx):
    t = jnp.arange(seq, dtype=jnp.int32)
    row = (t // GRID_W).astype(F32)
    col = (t % GRID_W).astype(F32)

    def axial(n):
        half = n // 2
        inv = ROPE_BASE ** (-2.0 * jnp.arange(half, dtype=F32) / n)
        ar, ac = row[:, None] * inv[None, :], col[:, None] * inv[None, :]
        cos = jnp.concatenate([jnp.cos(ar), jnp.cos(ar), jnp.cos(ac), jnp.cos(ac)], axis=-1)
        sin = jnp.concatenate([-jnp.sin(ar), jnp.sin(ar), -jnp.sin(ac), jnp.sin(ac)], axis=-1)
        cos = jnp.concatenate([cos, jnp.ones((n_ctx, 2 * n), F32)], axis=0)
        sin = jnp.concatenate([sin, jnp.zeros((n_ctx, 2 * n), F32)], axis=0)
        return cos, sin

    cos_a, sin_a = axial(DA_DIM // 2)
    cos_m, sin_m = axial(MLA_ROPE // 2)
    sa = DA_DIM ** -0.5 * LOG2E
    sm = (MLA_NOPE + MLA_ROPE) ** -0.5 * LOG2E
    tt = seq + n_ctx
    ones = jnp.ones((tt, MLA_NOPE), F32)
    zpad = jnp.zeros((tt, MLA_ZPAD), F32)
    return dict(
        caq=cos_a.T * sa, saq=sin_a.T * sa,
        cak=jnp.tile(cos_a, (1, 2)), sak=jnp.tile(sin_a, (1, 2)),
        cmq=cos_m.T * sm, smq=sin_m.T * sm,
        cmk=jnp.concatenate([ones, cos_m, zpad], axis=-1),
        smk=jnp.concatenate([0.0 * ones, sin_m, zpad], axis=-1))


def _layer_weights(w_in, gq, w_uq, gkv, w_ukv):
    d = w_in.shape[0]
    o = np.cumsum([0, 512, 512, 512, MLA_Q_RANK, MLA_KV_RANK, MLA_ROPE, 256, 256, 256])
    col = lambda k: w_in[:, int(o[k]):int(o[k + 1])]
    aq, ak, av, cq, ckv, kr, nq, nk, nv = (col(k) for k in range(9))
    kr_placed = jnp.concatenate([jnp.zeros((d, MLA_NOPE), w_in.dtype), kr,
                                 jnp.zeros((d, MLA_ZPAD), w_in.dtype)], axis=1)
    wtok = jnp.concatenate([ak] + [kr_placed] * N_HEADS + [ckv, nq, nv], axis=1).astype(BF16)
    wt = jnp.concatenate([aq, av, cq, ckv, nk], axis=1).T.astype(BF16)
    uq = w_uq.reshape(MLA_Q_RANK, N_HEADS, MLA_NOPE + MLA_ROPE)
    uq = jnp.pad(uq, ((0, 0), (0, 0), (0, MLA_ZPAD)))
    ukv = w_ukv.reshape(MLA_KV_RANK, N_HEADS, MLA_NOPE + MLA_VDIM)
    uk = jnp.pad(ukv[:, :, :MLA_NOPE], ((0, 0), (0, 0), (0, HEAD_PAD - MLA_NOPE)))
    return dict(
        wtok=wtok, wt=wt,
        gqt=gq.reshape(-1, 1).astype(F32),
        wuqt=uq.reshape(MLA_Q_RANK, N_HEADS * HEAD_PAD).T.astype(BF16),
        gkv=gkv.reshape(1, -1).astype(F32), gkvt=gkv.reshape(-1, 1).astype(F32),
        wuk=uk.reshape(MLA_KV_RANK, N_HEADS * HEAD_PAD).astype(BF16),
        wuvt=ukv[:, :, MLA_NOPE:].reshape(MLA_KV_RANK, N_HEADS * MLA_VDIM).T.astype(BF16))


def _pick_key_chunk(n_k):
    for n_c in range(2, n_k // FINE + 1, 2):
        if n_k % (n_c * FINE) == 0 and n_k // n_c <= ATTN_TK_MAX:
            return n_k // n_c
    raise ValueError(f"no even chunking of {n_k} keys")


def kernel(x, c, ctx, c_ctx, w_mod, b_mod, w_in, da_lam_q1, da_lam_k1, da_lam_q2, da_lam_k2, da_subln_g,
           mla_q_norm_g, mla_w_uq, mla_kv_norm_g, mla_w_ukv, na_rpb, w_out, ln1_g, ln1_b, w_ff1, w_ff2,
           ln2_g, ln2_b):
    b, seq, d = x.shape
    n_ctx = ctx.shape[1]
    depth = w_mod.shape[0]
    assert seq % TOK_TILE == 0 and n_ctx == TOK_TILE and b < 8 and seq % (NA_Q_ROWS * GRID_W) == 0
    alpha = (2.0 * depth) ** 0.25
    tq, tk = ATTN_TQ, _pick_key_chunk(seq + n_ctx)
    assert seq % tq == 0 and n_ctx % FINE == 0

    cs = jnp.zeros((8, d), F32).at[:b].set(c).at[b].set(c_ctx)
    mod = _modulation(cs, w_mod, b_mod).reshape(depth, 8, 6, d)
    tabs = _rope_tables(seq, n_ctx)
    xall = jnp.concatenate([x, ctx], axis=1)
    ctx_blk = seq // n_ctx

    for l in range(depth):
        last = l == depth - 1
        lam_init = 0.8 - 0.6 * math.exp(-0.3 * l)
        pw = _layer_weights(w_in[l], mla_q_norm_g[l], mla_w_uq[l], mla_kv_norm_g[l], mla_w_ukv[l])
        qat, ka, vat, qmt, km, vmt, nq, nkt, nv = _projection(xall, mod[l], pw, tabs, b)

        lam_vec = jnp.stack([da_lam_q1[l], da_lam_k1[l], da_lam_q2[l], da_lam_k2[l]]).astype(F32)
        g_sub = da_subln_g[l].reshape(-1, 1).astype(F32)
        ya = _dense_attention("da", qat, ka, vat, n_q=seq, tq=tq, tk=tk, q_blk0=0, k_blk0=0, n_k=seq + n_ctx,
                              extra=(lam_vec, g_sub), lam_init=lam_init)
        yb = _dense_attention("mla", qmt, km, vmt, n_q=seq, tq=tq, tk=tk, q_blk0=0, k_blk0=0, n_k=seq + n_ctx)
        yc = _neighbourhood_attention(nq, nkt, nv, _na_bias(na_rpb[l], seq // GRID_W), seq)
        if not last:
            ya = _dense_attention("da", qat, ka, vat, n_q=n_ctx, tq=n_ctx, tk=n_ctx, q_blk0=ctx_blk,
                                  k_blk0=ctx_blk, n_k=n_ctx, out=ya, extra=(lam_vec, g_sub), lam_init=lam_init)
            yb = _dense_attention("mla", qmt, km, vmt, n_q=n_ctx, tq=n_ctx, tk=n_ctx, q_blk0=ctx_blk,
                                  k_blk0=ctx_blk, n_k=n_ctx, out=yb)
            yc = _neighbourhood_attention_ctx(nq, nkt, nv, yc, seq)

        ln = jnp.stack([ln1_g[l], ln1_b[l], ln2_g[l], ln2_b[l]]).astype(F32)
        xall = _out_ffn(xall, ya, yb, yc, mod[l], w_out[l].astype(BF16), w_ff1[l].astype(BF16),
                        w_ff2[l].astype(BF16), ln, seq if last else seq + n_ctx, b, alpha)
    return xall
```

```python
import functools
import math

import numpy as np
import jax
import jax.numpy as jnp
from jax import lax
from jax.experimental import pallas as pl
from jax.experimental.pallas import tpu as pltpu

F32 = jnp.float32
BF16 = jnp.bfloat16

GRID_W = 64
ROPE_BASE = 10000.0
LN_EPS = 1e-6
RMS_EPS = 1e-6
SUBLN_EPS = 1e-5
N_HEADS = 4
DA_DIM = 64
DA_VDIM = 128
MLA_Q_RANK = 256
MLA_KV_RANK = 128
MLA_NOPE = 64
MLA_ROPE = 32
MLA_VDIM = 64
NA_DIM = 64
NA_KH = 8
NA_KW = 16
HEAD_PAD = 128
MLA_ZPAD = HEAD_PAD - MLA_NOPE - MLA_ROPE
ONES_ROWS = 16
DA_VROWS = DA_VDIM + ONES_ROWS
MLA_VROWS = MLA_VDIM + ONES_ROWS
NA_VROWS = NA_DIM + ONES_ROWS

LOG2E = math.log2(math.e)
NEG_BIG = -1e30

TOK_TILE = 256
ATTN_TQ = 256
ATTN_TK_MAX = 1664
FINE = 128
ROW_BLOCK = 64
NA_Q_ROWS = 4
NA_WIN_ROWS = NA_Q_ROWS + NA_KH
VMEM_LIMIT = 56 * 1024 * 1024


def _cparams(sem, flags=None):
    return pltpu.CompilerParams(dimension_semantics=sem, vmem_limit_bytes=VMEM_LIMIT, flags=flags)


def _dot(a, b):
    return jnp.dot(a, b, preferred_element_type=F32)


def _dot_exact(a, b):
    return jnp.dot(a, b, preferred_element_type=F32, precision=lax.Precision.HIGHEST)


def _mod_kernel(c_ref, w_ref, b_ref, o_ref):
    c = c_ref[...]
    s = c / (1.0 + jnp.exp(-c))
    o_ref[...] = _dot_exact(s, w_ref[...]) + b_ref[...]


def _modulation(cs, w_mod, b_mod):
    depth, d, n = w_mod.shape
    nb = 1024
    return pl.pallas_call(
        _mod_kernel,
        grid=(depth, n // nb),
        in_specs=[pl.BlockSpec((8, d), lambda l, j: (0, 0)),
                  pl.BlockSpec((None, d, nb), lambda l, j: (l, 0, j)),
                  pl.BlockSpec((None, 1, nb), lambda l, j: (l, 0, j))],
        out_specs=pl.BlockSpec((None, 8, nb), lambda l, j: (l, 0, j)),
        out_shape=jax.ShapeDtypeStruct((depth, 8, n), F32),
        compiler_params=_cparams(("arbitrary", "arbitrary")),
        name="modulation",
    )(cs, w_mod, b_mod.reshape(depth, 1, n))


def _rms(x, axis, eps):
    return x * lax.rsqrt(jnp.mean(jnp.square(x), axis=axis, keepdims=True) + eps)


def _rope_lanes(x, lane, half, cos, sin):
    partner = jnp.where((lane & half) == 0, pltpu.roll(x, HEAD_PAD - half, 1), pltpu.roll(x, half, 1))
    return x * cos + partner * sin


def _rope_rows(x, half, cos, sin):
    partner = jnp.concatenate([x[half:2 * half], x[0:half], x[3 * half:4 * half], x[2 * half:3 * half]], axis=0)
    return x * cos + partner * sin


def _proj_kernel(x_ref, mod_ref, wtok_ref, wt_ref,
                 caq_ref, saq_ref, cak_ref, sak_ref, cmq_ref, smq_ref, cmk_ref, smk_ref,
                 gqt_ref, wuqt_ref, gkv_ref, wuk_ref, gkvt_ref, wuvt_ref,
                 qat_ref, ka_ref, vat_ref, qmt_ref, km_ref, vmt_ref, nqt_ref, nk_ref, nvt_ref):
    x = x_ref[...]
    h = (x * (1.0 + mod_ref[1:2, :]) + mod_ref[0:1, :]).astype(BF16)
    p = _dot(h, wtok_ref[...])
    pt = lax.dot_general(wt_ref[...], h, (((1,), (1,)), ((), ())),
                         preferred_element_type=F32)
    tm = x.shape[0]
    lane = lax.broadcasted_iota(jnp.int32, (tm, HEAD_PAD), 1)

    caq, saq = caq_ref[...], saq_ref[...]
    for g in range(2 * N_HEADS):
        rows = slice(g * DA_DIM, (g + 1) * DA_DIM)
        qat_ref[rows, :] = _rope_rows(pt[rows, :], DA_DIM // 4, caq, saq).astype(BF16)
    cak, sak = cak_ref[...], sak_ref[...]
    for g in range(N_HEADS):
        cols = slice(g * HEAD_PAD, (g + 1) * HEAD_PAD)
        ka_ref[:, cols] = _rope_lanes(p[:, cols], lane, DA_DIM // 4, cak, sak).astype(BF16)
    ones = jnp.ones((ONES_ROWS, tm), BF16)
    for g in range(N_HEADS):
        r0 = g * DA_VROWS
        vat_ref[r0:r0 + DA_VDIM, :] = pt[512 + g * DA_VDIM:512 + (g + 1) * DA_VDIM, :].astype(BF16)
        vat_ref[r0 + DA_VDIM:r0 + DA_VROWS, :] = ones

    cqnt = (_rms(pt[1024:1280, :], 0, RMS_EPS) * gqt_ref[...]).astype(BF16)
    qmt = _dot(wuqt_ref[...], cqnt)
    cmq, smq = cmq_ref[...], smq_ref[...]
    scale_m = (MLA_NOPE + MLA_ROPE) ** -0.5 * LOG2E
    for g in range(N_HEADS):
        r0 = g * HEAD_PAD
        qmt_ref[r0:r0 + MLA_NOPE, :] = (qmt[r0:r0 + MLA_NOPE, :] * scale_m).astype(BF16)
        qmt_ref[r0 + MLA_NOPE:r0 + MLA_NOPE + MLA_ROPE, :] = _rope_rows(
            qmt[r0 + MLA_NOPE:r0 + MLA_NOPE + MLA_ROPE, :], MLA_ROPE // 4, cmq, smq).astype(BF16)
        qmt_ref[r0 + MLA_NOPE + MLA_ROPE:r0 + HEAD_PAD, :] = jnp.zeros((MLA_ZPAD, tm), BF16)
    ckvn = (_rms(p[:, 1024:1152], -1, RMS_EPS) * gkv_ref[...]).astype(BF16)
    km = _dot(ckvn, wuk_ref[...]) + p[:, 512:1024]
    cmk, smk = cmk_ref[...], smk_ref[...]
    for g in range(N_HEADS):
        cols = slice(g * HEAD_PAD, (g + 1) * HEAD_PAD)
        km_ref[:, cols] = _rope_lanes(km[:, cols], lane, MLA_ROPE // 4, cmk, smk).astype(BF16)
    ckvnt = (_rms(pt[1280:1408, :], 0, RMS_EPS) * gkvt_ref[...]).astype(BF16)
    vmt = _dot(wuvt_ref[...], ckvnt)
    for g in range(N_HEADS):
        r0 = g * MLA_VROWS
        vmt_ref[r0:r0 + MLA_VDIM, :] = vmt[g * MLA_VDIM:(g + 1) * MLA_VDIM, :].astype(BF16)
        vmt_ref[r0 + MLA_VDIM:r0 + MLA_VROWS, :] = ones

    nqt_ref[...] = (pt[1408:1664, :] * (NA_DIM ** -0.5 * LOG2E)).astype(BF16)
    nk_ref[...] = p[:, 1152:1408].astype(BF16)
    for g in range(N_HEADS):
        r0 = g * NA_VROWS
        nvt_ref[r0:r0 + NA_DIM, :] = pt[1664 + g * NA_DIM:1664 + (g + 1) * NA_DIM, :].astype(BF16)
        nvt_ref[r0 + NA_DIM:r0 + NA_VROWS, :] = ones


def _projection(xall, mod_l, pw, tabs, n_ctx_row):
    b, t, d = xall.shape
    tm = TOK_TILE
    nt = t // tm

    def tok(width):
        return pl.BlockSpec((None, tm, width), lambda j, i: (i, j, 0))

    def trn(height):
        return pl.BlockSpec((None, height, tm), lambda j, i: (i, 0, j))

    def const(arr):
        return pl.BlockSpec(arr.shape, lambda j, i: (0,) * arr.ndim)

    def tab_tok():
        return pl.BlockSpec((tm, HEAD_PAD), lambda j, i: (j, 0))

    def tab_trn(height):
        return pl.BlockSpec((height, tm), lambda j, i: (0, j))

    consts = [pw["gqt"], pw["wuqt"], pw["gkv"], pw["wuk"], pw["gkvt"], pw["wuvt"]]
    in_specs = [tok(d),
                pl.BlockSpec((None, 6, d), lambda j, i: (jnp.where(j == nt - 1, n_ctx_row, i), 0, 0)),
                const(pw["wtok"]), const(pw["wt"]),
                tab_trn(64), tab_trn(64), tab_tok(), tab_tok(),
                tab_trn(32), tab_trn(32), tab_tok(), tab_tok(),
                ] + [const(a) for a in consts]
    va_rows, vm_rows, vn_rows = N_HEADS * DA_VROWS, N_HEADS * MLA_VROWS, N_HEADS * NA_VROWS
    out_specs = [trn(512), tok(512), trn(va_rows), trn(512), tok(512), trn(vm_rows),
                 trn(256), tok(256), trn(vn_rows)]
    out_shape = [jax.ShapeDtypeStruct((b, 512, t), BF16), jax.ShapeDtypeStruct((b, t, 512), BF16),
                 jax.ShapeDtypeStruct((b, va_rows, t), BF16), jax.ShapeDtypeStruct((b, 512, t), BF16),
                 jax.ShapeDtypeStruct((b, t, 512), BF16), jax.ShapeDtypeStruct((b, vm_rows, t), BF16),
                 jax.ShapeDtypeStruct((b, 256, t), BF16), jax.ShapeDtypeStruct((b, t, 256), BF16),
                 jax.ShapeDtypeStruct((b, vn_rows, t), BF16)]
    return pl.pallas_call(
        _proj_kernel,
        grid=(nt, b),
        in_specs=in_specs,
        out_specs=out_specs,
        out_shape=out_shape,
        compiler_params=_cparams(("parallel", "parallel")),
        name="projection",
    )(xall, mod_l, pw["wtok"], pw["wt"],
      tabs["caq"], tabs["saq"], tabs["cak"], tabs["sak"], tabs["cmq"], tabs["smq"], tabs["cmk"], tabs["smk"],
      *consts)


def _fold_rows(op, x, rows):
    n = x.shape[0]
    if n > ROW_BLOCK:
        acc = x[0:ROW_BLOCK]
        for g in range(1, n // ROW_BLOCK):
            acc = op(acc, x[g * ROW_BLOCK:(g + 1) * ROW_BLOCK])
        x, n = acc, ROW_BLOCK
    while n > rows:
        n //= 2
        x = op(x[:n], x[n:])
    return x


def _attn_kernel(*refs, mode, tq, n_k, tk, lam_init, aliased_out):
    o_ref, m_ref, acc_ref, s_ref, cm_ref, p_ref, al_ref = refs[-7:]
    refs = refs[:-8] if aliased_out else refs[:-7]
    if mode == "da":
        qt_ref, k_ref, vt_ref, lam_ref, g_ref = refs
        kcols = ((0, 128), (0, 128))
        vdim, vrows = DA_VDIM, ((0, DA_VROWS), (0, DA_VROWS))
    else:
        qt_ref, k_ref, vt_ref = refs
        kcols = ((0, 128), (128, 256))
        vdim, vrows = MLA_VDIM, ((0, MLA_VROWS), (MLA_VROWS, 2 * MLA_VROWS))
    n_t, n_c = qt_ref.shape[1] // tq, n_k // tk
    total = n_t * n_c
    assert n_k % tk == 0 and tk % FINE == 0 and (total == 1 or n_c % 2 == 0)

    m_ref[...] = jnp.full(m_ref.shape, NEG_BIG, F32)
    acc_ref[...] = jnp.zeros(acc_ref.shape, F32)

    def query_maps(tile):
        col = tile * tq if isinstance(tile, int) else pl.multiple_of(tile * tq, tq)
        if mode == "da":
            qt = qt_ref[:, pl.ds(col, tq)]
            row = lax.broadcasted_iota(jnp.int32, qt.shape, 0)
            zero = jnp.zeros_like(qt)
            return (jnp.where(row < DA_DIM, qt, zero), jnp.where(row >= DA_DIM, qt, zero))
        return (qt_ref[0:128, pl.ds(col, tq)], qt_ref[128:256, pl.ds(col, tq)])

    def split(f):
        if isinstance(f, int):
            return f // n_c, (f % n_c) * tk
        tile = lax.div(f, jnp.int32(n_c))
        return tile, pl.multiple_of((f - tile * n_c) * tk, 128)

    def stage(f, par):
        static = isinstance(f, int)
        do_a = not static or f < total
        do_b = not static or 1 <= f <= total
        do_c = not static or 2 <= f <= total + 1
        if do_a:
            a_tile, a_start = split(f)
            qts = query_maps(a_tile)
        if do_b:
            _, b_start = split(f - 1)
        if do_c:
            _, c_start = split(f - 2)
        for i in range(2):
            if do_b:
                m_prev = jnp.where(b_start == 0, NEG_BIG, m_ref[i])
                m_new = jnp.maximum(m_prev, cm_ref[1 - par, i])
            if do_c:
                pv = _dot(vt_ref[vrows[i][0]:vrows[i][1], pl.ds(c_start, tk)], p_ref[par, i])
                acc_ref[i] = al_ref[par, i] * acc_ref[i] + pv
            if do_a:
                st = _dot(k_ref[pl.ds(a_start, tk), kcols[i][0]:kcols[i][1]], qts[i])
                s_ref[par, i] = st
                cm_ref[par, i] = jnp.max(_fold_rows(jnp.maximum, st, 8), axis=0, keepdims=True)
            if do_b:
                for r in range(0, tk, ROW_BLOCK):
                    x = s_ref[1 - par, i, r:r + ROW_BLOCK, :] - m_new
                    p_ref[1 - par, i, r:r + ROW_BLOCK, :] = jnp.exp2(x.astype(BF16))
                al_ref[1 - par, i] = jnp.exp2(m_prev - m_new)
                m_ref[i] = m_new

    def finish_tile(tile):
        o0 = acc_ref[0, 0:vdim, :] / acc_ref[0, vdim:vdim + 1, :]
        o1 = acc_ref[1, 0:vdim, :] / acc_ref[1, vdim:vdim + 1, :]
        if mode == "da":
            lv = lam_ref[...]
            lam = (jnp.exp(jnp.sum(lv[0:1] * lv[1:2], axis=-1, keepdims=True))
                   - jnp.exp(jnp.sum(lv[2:3] * lv[3:4], axis=-1, keepdims=True)) + lam_init)
            o = o0 - lam * o1
            o = _rms(o, 0, SUBLN_EPS) * g_ref[...] * (1.0 - lam_init)
        else:
            o = jnp.concatenate([o0, o1], axis=0)
        row = tile * tq if isinstance(tile, int) else pl.multiple_of(tile * tq, tq)
        o_ref[pl.ds(row, tq), :] = o.T.astype(o_ref.dtype)

    def static_stage(f):
        stage(f, f % 2)
        if f >= 2 and (f - 2) % n_c == n_c - 1:
            finish_tile((f - 2) // n_c)

    for f in range(0, min(2, total + 2)):
        static_stage(f)

    def body(j, carry):
        f0 = 2 * j
        stage(f0, 0)
        stage(f0 + 1, 1)

        @pl.when(lax.rem(f0 - 1, jnp.int32(n_c)) == n_c - 1)
        def _():
            finish_tile(lax.div(f0 - 1, jnp.int32(n_c)))
        return carry

    if total > 2:
        lax.fori_loop(1, total // 2, body, 0)
    for f in range(max(2, total), total + 2):
        static_stage(f)


def _dense_attention(mode, qt, k, vt, *, n_q, tq, tk, q_blk0, k_blk0, n_k, out=None, out_rows=None,
                     extra=(), lam_init=0.0):
    b, _, t = qt.shape
    groups = N_HEADS if mode == "da" else N_HEADS // 2
    qw = 128 if mode == "da" else 256
    vblk = DA_VROWS if mode == "da" else 2 * MLA_VROWS
    acc_rows = DA_VROWS if mode == "da" else MLA_VROWS
    aliased = out is not None
    out_shape = jax.ShapeDtypeStruct(out.shape if aliased else (b, out_rows, 128 * groups), BF16)
    in_specs = [pl.BlockSpec((None, qw, n_q), lambda i, g: (i, g, q_blk0)),
                pl.BlockSpec((None, n_k, qw), lambda i, g: (i, k_blk0, g)),
                pl.BlockSpec((None, vblk, n_k), lambda i, g: (i, g, k_blk0))]
    args = [qt, k, vt]
    for a in extra:
        in_specs.append(pl.BlockSpec(a.shape, lambda i, g: (0, 0)))
        args.append(a)
    alias = {}
    if aliased:
        in_specs.append(pl.BlockSpec(memory_space=pl.ANY))
        alias = {len(args): 0}
        args.append(out)
    return pl.pallas_call(
        functools.partial(_attn_kernel, mode=mode, tq=tq, n_k=n_k, tk=tk, lam_init=lam_init,
                          aliased_out=aliased),
        grid=(b, groups),
        in_specs=in_specs,
        out_specs=pl.BlockSpec((None, n_q, 128), lambda i, g: (i, q_blk0, g)),
        out_shape=out_shape,
        scratch_shapes=[pltpu.VMEM((2, 1, tq), F32),
                        pltpu.VMEM((2, acc_rows, tq), F32), pltpu.VMEM((2, 2, tk, tq), F32),
                        pltpu.VMEM((2, 2, 1, tq), F32), pltpu.VMEM((2, 2, tk, tq), BF16),
                        pltpu.VMEM((2, 2, 1, tq), F32)],
        input_output_aliases=alias,
        compiler_params=_cparams(("parallel", "parallel")),
        name="attn_" + mode,
    )(*args)


def _na_kernel(*refs, rows, ctx_off, n_ctx, use_window):
    if use_window:
        qt_ref, k_ref, vt_ref, bias_ref, o_ref, s_ref, p_ref = refs
    else:
        qt_ref, k_ref, vt_ref, o_ref, s_ref, p_ref = refs
    qt = qt_ref[...]
    row = lax.broadcasted_iota(jnp.int32, qt.shape, 0)
    zero = jnp.zeros_like(qt)
    n_win = NA_WIN_ROWS * GRID_W if use_window else 0
    if use_window:
        i = pl.program_id(1)
        kr0 = jnp.clip(i * NA_Q_ROWS - NA_KH // 2, 0, rows - NA_WIN_ROWS)
        start = pl.multiple_of(kr0 * GRID_W, 128)
    maxima, outs = [], []
    for h in range(N_HEADS):
        qh = jnp.where((row >= h * NA_DIM) & (row < (h + 1) * NA_DIM), qt, zero)
        s_c = _dot(k_ref[ctx_off:ctx_off + n_ctx, :], qh)
        s_ref[h, n_win:n_win + n_ctx, :] = s_c
        cmax = _fold_rows(jnp.maximum, s_c, 8)
        if use_window:
            s_w = _dot(k_ref[pl.ds(start, n_win), :], qh) + bias_ref[h]
            s_ref[h, 0:n_win, :] = s_w
            cmax = jnp.maximum(cmax, _fold_rows(jnp.maximum, s_w, 8))
        maxima.append(jnp.max(cmax, axis=0, keepdims=True))
    for h in range(N_HEADS):
        for r in range(0, n_win + n_ctx, ROW_BLOCK):
            p_ref[h, r:r + ROW_BLOCK, :] = jnp.exp2((s_ref[h, r:r + ROW_BLOCK, :] - maxima[h]).astype(BF16))
    for h in range(N_HEADS):
        vrows = slice(h * NA_VROWS, (h + 1) * NA_VROWS)
        o = _dot(vt_ref[vrows, ctx_off:ctx_off + n_ctx], p_ref[h, n_win:n_win + n_ctx, :])
        if use_window:
            o = o + _dot(vt_ref[vrows, pl.ds(start, n_win)], p_ref[h, 0:n_win, :])
        outs.append(o[0:NA_DIM] / o[NA_DIM:NA_DIM + 1])
    o_ref[...] = jnp.concatenate(outs, axis=0).T.astype(o_ref.dtype)


def _na_bias_index(rows):
    nb = rows // NA_Q_ROWS
    assert rows % NA_Q_ROWS == 0 and nb >= 3 and rows >= NA_WIN_ROWS and (rows - NA_WIN_ROWS) % 2 == 0

    def block(i):
        r = i * NA_Q_ROWS + np.arange(NA_Q_ROWS)
        kr = int(np.clip(i * NA_Q_ROWS - NA_KH // 2, 0, rows - NA_WIN_ROWS)) + np.arange(NA_WIN_ROWS)
        rs = np.clip(r - NA_KH // 2, 0, rows - NA_KH)
        valid = (kr[None, :] >= rs[:, None]) & (kr[None, :] < rs[:, None] + NA_KH)
        dy = np.clip(kr[None, :] - r[:, None] + NA_KH - 1, 0, 2 * NA_KH - 2)
        return valid, dy

    blocks = [block(i) for i in range(nb)]
    for i in range(2, nb - 1):
        assert all(np.array_equal(a, b_) for a, b_ in zip(blocks[1], blocks[i]))
    cls = [blocks[0], blocks[1], blocks[nb - 1]]
    rvalid = np.stack([c[0] for c in cls]).reshape(-1)
    dy = np.stack([c[1] for c in cls]).reshape(-1)
    rsel = (dy[:, None] == np.arange(2 * NA_KH - 1)[None, :]).astype(np.float32)

    c = np.arange(GRID_W)
    cs = np.clip(c - NA_KW // 2, 0, GRID_W - NA_KW)
    cvalid = ((c[None, :] >= cs[:, None]) & (c[None, :] < cs[:, None] + NA_KW)).reshape(-1)
    dx = np.clip(c[None, :] - c[:, None] + NA_KW - 1, 0, 2 * NA_KW - 2).reshape(-1)
    csel = (np.arange(2 * NA_KW - 1)[:, None] == dx[None, :]).astype(np.float32)
    return rsel, rvalid.astype(np.float32)[:, None], csel, cvalid.astype(np.float32)[None, :]


def _na_bias_kernel(rpb_ref, rsel_ref, rmask_ref, csel_ref, cmask_ref, o_ref):
    cols = _dot_exact(rpb_ref[...], csel_ref[...])
    vals = _dot_exact(rsel_ref[...], cols)
    valid = (rmask_ref[...] * cmask_ref[...]) > 0.5
    o_ref[...] = jnp.where(valid, vals * LOG2E, NEG_BIG)


def _na_bias(rpb, rows):
    rsel, rmask, csel, cmask = (jnp.asarray(a) for a in _na_bias_index(rows))
    nh = rpb.shape[0]
    n_r, n_c = rsel.shape[0], csel.shape[1]
    full = lambda a: pl.BlockSpec(a.shape, lambda h: (0, 0))
    out = pl.pallas_call(
        _na_bias_kernel,
        grid=(nh,),
        in_specs=[pl.BlockSpec((None,) + rpb.shape[1:], lambda h: (h, 0, 0)),
                  full(rsel), full(rmask), full(csel), full(cmask)],
        out_specs=pl.BlockSpec((None, n_r, n_c), lambda h: (h, 0, 0)),
        out_shape=jax.ShapeDtypeStruct((nh, n_r, n_c), F32),
        compiler_params=_cparams(("parallel",)),
        name="na_bias",
    )(rpb.astype(F32), rsel, rmask, csel, cmask)
    out = out.reshape(nh, 3, NA_Q_ROWS, NA_WIN_ROWS, GRID_W, GRID_W)
    out = jnp.transpose(out, (1, 0, 3, 5, 2, 4))
    return out.reshape(3, nh, NA_WIN_ROWS * GRID_W, NA_Q_ROWS * GRID_W)


def _neighbourhood_attention(nqt, nk, nvt, bias, seq, out_rows):
    b, t, w = nk.shape
    tq = NA_Q_ROWS * GRID_W
    nb = seq // tq
    rows = seq // GRID_W
    n_keys = NA_WIN_ROWS * GRID_W + t - seq
    return pl.pallas_call(
        functools.partial(_na_kernel, rows=rows, ctx_off=seq, n_ctx=t - seq, use_window=True),
        grid=(b, nb),
        in_specs=[pl.BlockSpec((None, w, tq), lambda i, j: (i, 0, j)),
                  pl.BlockSpec((None, t, w), lambda i, j: (i, 0, 0)),
                  pl.BlockSpec((None, nvt.shape[1], t), lambda i, j: (i, 0, 0)),
                  pl.BlockSpec((None,) + bias.shape[1:],
                               lambda i, j: (jnp.where(j == 0, 0, jnp.where(j == nb - 1, 2, 1)), 0, 0, 0))],
        out_specs=pl.BlockSpec((None, tq, w), lambda i, j: (i, j, 0)),
        out_shape=jax.ShapeDtypeStruct((b, out_rows, w), BF16),
        scratch_shapes=[pltpu.VMEM((N_HEADS, n_keys, tq), F32), pltpu.VMEM((N_HEADS, n_keys, tq), BF16)],
        compiler_params=_cparams(("parallel", "parallel")),
        name="attn_na",
    )(nqt, nk, nvt, bias)


def _neighbourhood_attention_ctx(nqt, nk, nvt, out, seq):
    b, t, w = nk.shape
    n_ctx = t - seq
    blk = seq // n_ctx
    kern = lambda q, k, v, _, o, *scratch, **kw: _na_kernel(q, k, v, o, *scratch, **kw)
    return pl.pallas_call(
        functools.partial(kern, rows=0, ctx_off=0, n_ctx=n_ctx, use_window=False),
        grid=(b,),
        in_specs=[pl.BlockSpec((None, w, n_ctx), lambda i: (i, 0, blk)),
                  pl.BlockSpec((None, n_ctx, w), lambda i: (i, blk, 0)),
                  pl.BlockSpec((None, nvt.shape[1], n_ctx), lambda i: (i, 0, blk)),
                  pl.BlockSpec(memory_space=pl.ANY)],
        out_specs=pl.BlockSpec((None, n_ctx, w), lambda i: (i, blk, 0)),
        out_shape=jax.ShapeDtypeStruct(out.shape, out.dtype),
        scratch_shapes=[pltpu.VMEM((N_HEADS, n_ctx, n_ctx), F32), pltpu.VMEM((N_HEADS, n_ctx, n_ctx), BF16)],
        input_output_aliases={3: 0},
        compiler_params=_cparams(("parallel",)),
        name="attn_na_ctx",
    )(nqt, nk, nvt, out)


def _layer_norm(z, g, b):
    mu = jnp.mean(z, axis=-1, keepdims=True)
    zc = z - mu
    var = jnp.mean(jnp.square(zc), axis=-1, keepdims=True)
    return zc * lax.rsqrt(var + LN_EPS) * g + b


def _ffn_kernel(x_ref, ya_ref, yb_ref, yc_ref, mod_ref, wo_ref, w1_ref, w2_ref, ln_ref, o_ref, *, alpha):
    x = x_ref[...]
    a = (_dot(ya_ref[...], wo_ref[0:512, :]) + _dot(yb_ref[...], wo_ref[512:768, :])
         + _dot(yc_ref[...], wo_ref[768:1024, :]))
    x1 = _layer_norm(alpha * x + mod_ref[2:3, :] * a, ln_ref[0:1, :], ln_ref[1:2, :])
    h2 = (x1 * (1.0 + mod_ref[4:5, :]) + mod_ref[3:4, :]).astype(BF16)
    u = jnp.maximum(_dot(h2, w1_ref[...]), 0.0)
    f = _dot(jnp.square(u).astype(BF16), w2_ref[...])
    o_ref[...] = _layer_norm(alpha * x1 + mod_ref[5:6, :] * f, ln_ref[2:3, :], ln_ref[3:4, :])


def _out_ffn(xall, ya, yb, yc, mod_l, wo, w1, w2, ln, n_tok, n_ctx_row, alpha):
    b, t, d = xall.shape
    tm = TOK_TILE
    nt_all = t // tm
    nt = n_tok // tm

    def tok(width):
        return pl.BlockSpec((None, tm, width), lambda i, j: (i, j, 0))

    def const(arr):
        return pl.BlockSpec(arr.shape, lambda i, j: (0,) * arr.ndim, pipeline_mode=pl.Buffered(1))

    return pl.pallas_call(
        functools.partial(_ffn_kernel, alpha=alpha),
        grid=(b, nt),
        in_specs=[tok(d), tok(512), tok(256), tok(256),
                  pl.BlockSpec((None, 6, d), lambda i, j: (jnp.where(j == nt_all - 1, n_ctx_row, i), 0, 0)),
                  const(wo), const(w1), const(w2), const(ln)],
        out_specs=tok(d),
        out_shape=jax.ShapeDtypeStruct((b, n_tok, d), F32),
        compiler_params=_cparams(("parallel", "parallel")),
        name="out_ffn",
    )(xall, ya, yb, yc, mod_l, wo, w1, w2, ln)


def _rope_tables(seq, n_ctx):
    t = jnp.arange(seq, dtype=jnp.int32)
    row = (t // GRID_W).astype(F32)
    col = (t % GRID_W).astype(F32)

    def axial(n):
        half = n // 2
        inv = ROPE_BASE ** (-2.0 * jnp.arange(half, dtype=F32) / n)
        ar, ac = row[:, None] * inv[None, :], col[:, None] * inv[None, :]
        cos = jnp.concatenate([jnp.cos(ar), jnp.cos(ar), jnp.cos(ac), jnp.cos(ac)], axis=-1)
        sin = jnp.concatenate([-jnp.sin(ar), jnp.sin(ar), -jnp.sin(ac), jnp.sin(ac)], axis=-1)
        cos = jnp.concatenate([cos, jnp.ones((n_ctx, 2 * n), F32)], axis=0)
        sin = jnp.concatenate([sin, jnp.zeros((n_ctx, 2 * n), F32)], axis=0)
        return cos, sin

    cos_a, sin_a = axial(DA_DIM // 2)
    cos_m, sin_m = axial(MLA_ROPE // 2)
    sa = DA_DIM ** -0.5 * LOG2E
    sm = (MLA_NOPE + MLA_ROPE) ** -0.5 * LOG2E
    tt = seq + n_ctx
    ones = jnp.ones((tt, MLA_NOPE), F32)
    zpad = jnp.zeros((tt, MLA_ZPAD), F32)
    return dict(
        caq=cos_a.T * sa, saq=sin_a.T * sa,
        cak=jnp.tile(cos_a, (1, 2)), sak=jnp.tile(sin_a, (1, 2)),
        cmq=cos_m.T * sm, smq=sin_m.T * sm,
        cmk=jnp.concatenate([ones, cos_m, zpad], axis=-1),
        smk=jnp.concatenate([0.0 * ones, sin_m, zpad], axis=-1))


def _layer_weights(w_in, gq, w_uq, gkv, w_ukv):
    d = w_in.shape[0]
    o = np.cumsum([0, 512, 512, 512, MLA_Q_RANK, MLA_KV_RANK, MLA_ROPE, 256, 256, 256])
    col = lambda k: w_in[:, int(o[k]):int(o[k + 1])]
    aq, ak, av, cq, ckv, kr, nq, nk, nv = (col(k) for k in range(9))
    kr_placed = jnp.concatenate([jnp.zeros((d, MLA_NOPE), w_in.dtype), kr,
                                 jnp.zeros((d, MLA_ZPAD), w_in.dtype)], axis=1)
    wtok = jnp.concatenate([ak] + [kr_placed] * N_HEADS + [ckv, nk], axis=1).astype(BF16)
    wt = jnp.concatenate([aq, av, cq, ckv, nq, nv], axis=1).T.astype(BF16)
    uq = w_uq.reshape(MLA_Q_RANK, N_HEADS, MLA_NOPE + MLA_ROPE)
    uq = jnp.pad(uq, ((0, 0), (0, 0), (0, MLA_ZPAD)))
    ukv = w_ukv.reshape(MLA_KV_RANK, N_HEADS, MLA_NOPE + MLA_VDIM)
    uk = jnp.pad(ukv[:, :, :MLA_NOPE], ((0, 0), (0, 0), (0, HEAD_PAD - MLA_NOPE)))
    return dict(
        wtok=wtok, wt=wt,
        gqt=gq.reshape(-1, 1).astype(F32),
        wuqt=uq.reshape(MLA_Q_RANK, N_HEADS * HEAD_PAD).T.astype(BF16),
        gkv=gkv.reshape(1, -1).astype(F32), gkvt=gkv.reshape(-1, 1).astype(F32),
        wuk=uk.reshape(MLA_KV_RANK, N_HEADS * HEAD_PAD).astype(BF16),
        wuvt=ukv[:, :, MLA_NOPE:].reshape(MLA_KV_RANK, N_HEADS * MLA_VDIM).T.astype(BF16))


def _pick_key_chunk(n_k):
    for n_c in range(2, n_k // FINE + 1, 2):
        if n_k % (n_c * FINE) == 0 and n_k // n_c <= ATTN_TK_MAX:
            return n_k // n_c
    raise ValueError(f"no even chunking of {n_k} keys")


def kernel(x, c, ctx, c_ctx, w_mod, b_mod, w_in, da_lam_q1, da_lam_k1, da_lam_q2, da_lam_k2, da_subln_g,
           mla_q_norm_g, mla_w_uq, mla_kv_norm_g, mla_w_ukv, na_rpb, w_out, ln1_g, ln1_b, w_ff1, w_ff2,
           ln2_g, ln2_b):
    b, seq, d = x.shape
    n_ctx = ctx.shape[1]
    depth = w_mod.shape[0]
    assert seq % TOK_TILE == 0 and n_ctx == TOK_TILE and b < 8 and seq % (NA_Q_ROWS * GRID_W) == 0
    alpha = (2.0 * depth) ** 0.25
    tq, tk = ATTN_TQ, _pick_key_chunk(seq + n_ctx)
    assert seq % tq == 0 and n_ctx % FINE == 0

    cs = jnp.zeros((8, d), F32).at[:b].set(c).at[b].set(c_ctx)
    mod = _modulation(cs, w_mod, b_mod).reshape(depth, 8, 6, d)
    tabs = _rope_tables(seq, n_ctx)
    xall = jnp.concatenate([x, ctx], axis=1)
    ctx_blk = seq // n_ctx

    for l in range(depth):
        last = l == depth - 1
        lam_init = 0.8 - 0.6 * math.exp(-0.3 * l)
        pw = _layer_weights(w_in[l], mla_q_norm_g[l], mla_w_uq[l], mla_kv_norm_g[l], mla_w_ukv[l])
        qat, ka, vat, qmt, km, vmt, nqt, nk, nvt = _projection(xall, mod[l], pw, tabs, b)

        lam_vec = jnp.stack([da_lam_q1[l], da_lam_k1[l], da_lam_q2[l], da_lam_k2[l]]).astype(F32)
        g_sub = da_subln_g[l].reshape(-1, 1).astype(F32)
        n_out = seq if last else seq + n_ctx
        ya = _dense_attention("da", qat, ka, vat, n_q=seq, tq=tq, tk=tk, q_blk0=0, k_blk0=0, n_k=seq + n_ctx,
                              out_rows=n_out, extra=(lam_vec, g_sub), lam_init=lam_init)
        yb = _dense_attention("mla", qmt, km, vmt, n_q=seq, tq=tq, tk=tk, q_blk0=0, k_blk0=0, n_k=seq + n_ctx,
                              out_rows=n_out)
        yc = _neighbourhood_attention(nqt, nk, nvt, _na_bias(na_rpb[l], seq // GRID_W), seq, n_out)
        if not last:
            ya = _dense_attention("da", qat, ka, vat, n_q=n_ctx, tq=n_ctx, tk=n_ctx, q_blk0=ctx_blk,
                                  k_blk0=ctx_blk, n_k=n_ctx, out=ya, extra=(lam_vec, g_sub), lam_init=lam_init)
            yb = _dense_attention("mla", qmt, km, vmt, n_q=n_ctx, tq=n_ctx, tk=n_ctx, q_blk0=ctx_blk,
                                  k_blk0=ctx_blk, n_k=n_ctx, out=yb)
            yc = _neighbourhood_attention_ctx(nqt, nk, nvt, yc, seq)

        ln = jnp.stack([ln1_g[l], ln1_b[l], ln2_g[l], ln2_b[l]]).astype(F32)
        xall = _out_ffn(xall, ya, yb, yc, mod[l], w_out[l].astype(BF16), w_ff1[l].astype(BF16),
                        w_ff2[l].astype(BF16), ln, n_out, b, alpha)
    return xall
```

```python
import functools
import math

import numpy as np
import jax
import jax.numpy as jnp
from jax import lax
from jax.experimental import pallas as pl
from jax.experimental.pallas import tpu as pltpu

F32 = jnp.float32
BF16 = jnp.bfloat16

GRID_W = 64
ROPE_BASE = 10000.0
LN_EPS = 1e-6
RMS_EPS = 1e-6
SUBLN_EPS = 1e-5
N_HEADS = 4
DA_DIM = 64
DA_VDIM = 128
MLA_Q_RANK = 256
MLA_KV_RANK = 128
MLA_NOPE = 64
MLA_ROPE = 32
MLA_VDIM = 64
NA_DIM = 64
NA_KH = 8
NA_KW = 16
HEAD_PAD = 128
MLA_ZPAD = HEAD_PAD - MLA_NOPE - MLA_ROPE
ONES_ROWS = 16
DA_VROWS = DA_VDIM + ONES_ROWS
MLA_VROWS = MLA_VDIM + ONES_ROWS
NA_VROWS = NA_DIM + ONES_ROWS

LOG2E = math.log2(math.e)
NEG_BIG = -1e30

TOK_TILE = 256
ATTN_TQ = 256
DA_TK_MAX = 4224
MLA_TK_MAX = 1664
FINE = 128
ROW_BLOCK = 64
NA_Q_ROWS = 4
NA_WIN_ROWS = NA_Q_ROWS + NA_KH
VMEM_LIMIT = 56 * 1024 * 1024


def _cparams(sem, flags=None):
    return pltpu.CompilerParams(dimension_semantics=sem, vmem_limit_bytes=VMEM_LIMIT, flags=flags)


def _dot(a, b):
    return jnp.dot(a, b, preferred_element_type=F32)


def _dot_exact(a, b):
    return jnp.dot(a, b, preferred_element_type=F32, precision=lax.Precision.HIGHEST)


def _mod_kernel(c_ref, w_ref, b_ref, o_ref):
    c = c_ref[...]
    s = c / (1.0 + jnp.exp(-c))
    o_ref[...] = _dot_exact(s, w_ref[...]) + b_ref[...]


def _modulation(cs, w_mod, b_mod):
    depth, d, n = w_mod.shape
    nb = 1024
    return pl.pallas_call(
        _mod_kernel,
        grid=(depth, n // nb),
        in_specs=[pl.BlockSpec((8, d), lambda l, j: (0, 0)),
                  pl.BlockSpec((None, d, nb), lambda l, j: (l, 0, j)),
                  pl.BlockSpec((None, 1, nb), lambda l, j: (l, 0, j))],
        out_specs=pl.BlockSpec((None, 8, nb), lambda l, j: (l, 0, j)),
        out_shape=jax.ShapeDtypeStruct((depth, 8, n), F32),
        compiler_params=_cparams(("arbitrary", "arbitrary")),
        name="modulation",
    )(cs, w_mod, b_mod.reshape(depth, 1, n))


def _rms(x, axis, eps):
    return x * lax.rsqrt(jnp.mean(jnp.square(x), axis=axis, keepdims=True) + eps)


def _rope_lanes(x, lane, half, cos, sin):
    partner = jnp.where((lane & half) == 0, pltpu.roll(x, HEAD_PAD - half, 1), pltpu.roll(x, half, 1))
    return x * cos + partner * sin


def _rope_rows(x, half, cos, sin):
    partner = jnp.concatenate([x[half:2 * half], x[0:half], x[3 * half:4 * half], x[2 * half:3 * half]], axis=0)
    return x * cos + partner * sin


def _proj_kernel(x_ref, mod_ref, wtok_ref, wt_ref,
                 caq_ref, saq_ref, cak_ref, sak_ref, cmq_ref, smq_ref, cmk_ref, smk_ref,
                 gqt_ref, wuqt_ref, gkv_ref, wuk_ref, gkvt_ref, wuvt_ref,
                 qat_ref, ka_ref, vat_ref, qmt_ref, km_ref, vmt_ref, nqt_ref, nk_ref, nvt_ref):
    x = x_ref[...]
    h = (x * (1.0 + mod_ref[1:2, :]) + mod_ref[0:1, :]).astype(BF16)
    p = _dot(h, wtok_ref[...])
    pt = lax.dot_general(wt_ref[...], h, (((1,), (1,)), ((), ())),
                         preferred_element_type=F32)
    tm = x.shape[0]
    lane = lax.broadcasted_iota(jnp.int32, (tm, HEAD_PAD), 1)

    caq, saq = caq_ref[...], saq_ref[...]
    for g in range(2 * N_HEADS):
        rows = slice(g * DA_DIM, (g + 1) * DA_DIM)
        qat_ref[rows, :] = _rope_rows(pt[rows, :], DA_DIM // 4, caq, saq).astype(BF16)
    cak, sak = cak_ref[...], sak_ref[...]
    for g in range(N_HEADS):
        cols = slice(g * HEAD_PAD, (g + 1) * HEAD_PAD)
        ka_ref[:, cols] = _rope_lanes(p[:, cols], lane, DA_DIM // 4, cak, sak).astype(BF16)
    ones = jnp.ones((ONES_ROWS, tm), BF16)
    for g in range(N_HEADS):
        r0 = g * DA_VROWS
        vat_ref[r0:r0 + DA_VDIM, :] = pt[512 + g * DA_VDIM:512 + (g + 1) * DA_VDIM, :].astype(BF16)
        vat_ref[r0 + DA_VDIM:r0 + DA_VROWS, :] = ones

    cqnt = (_rms(pt[1024:1280, :], 0, RMS_EPS) * gqt_ref[...]).astype(BF16)
    qmt = _dot(wuqt_ref[...], cqnt)
    cmq, smq = cmq_ref[...], smq_ref[...]
    scale_m = (MLA_NOPE + MLA_ROPE) ** -0.5 * LOG2E
    for g in range(N_HEADS):
        r0 = g * HEAD_PAD
        qmt_ref[r0:r0 + MLA_NOPE, :] = (qmt[r0:r0 + MLA_NOPE, :] * scale_m).astype(BF16)
        qmt_ref[r0 + MLA_NOPE:r0 + MLA_NOPE + MLA_ROPE, :] = _rope_rows(
            qmt[r0 + MLA_NOPE:r0 + MLA_NOPE + MLA_ROPE, :], MLA_ROPE // 4, cmq, smq).astype(BF16)
        qmt_ref[r0 + MLA_NOPE + MLA_ROPE:r0 + HEAD_PAD, :] = jnp.zeros((MLA_ZPAD, tm), BF16)
    ckvn = (_rms(p[:, 1024:1152], -1, RMS_EPS) * gkv_ref[...]).astype(BF16)
    km = _dot(ckvn, wuk_ref[...]) + p[:, 512:1024]
    cmk, smk = cmk_ref[...], smk_ref[...]
    for g in range(N_HEADS):
        cols = slice(g * HEAD_PAD, (g + 1) * HEAD_PAD)
        km_ref[:, cols] = _rope_lanes(km[:, cols], lane, MLA_ROPE // 4, cmk, smk).astype(BF16)
    ckvnt = (_rms(pt[1280:1408, :], 0, RMS_EPS) * gkvt_ref[...]).astype(BF16)
    vmt = _dot(wuvt_ref[...], ckvnt)
    for g in range(N_HEADS):
        r0 = g * MLA_VROWS
        vmt_ref[r0:r0 + MLA_VDIM, :] = vmt[g * MLA_VDIM:(g + 1) * MLA_VDIM, :].astype(BF16)
        vmt_ref[r0 + MLA_VDIM:r0 + MLA_VROWS, :] = ones

    nqt_ref[...] = (pt[1408:1664, :] * (NA_DIM ** -0.5 * LOG2E)).astype(BF16)
    nk_ref[...] = p[:, 1152:1408].astype(BF16)
    for g in range(N_HEADS):
        r0 = g * NA_VROWS
        nvt_ref[r0:r0 + NA_DIM, :] = pt[1664 + g * NA_DIM:1664 + (g + 1) * NA_DIM, :].astype(BF16)
        nvt_ref[r0 + NA_DIM:r0 + NA_VROWS, :] = ones


def _projection(xall, mod_l, pw, tabs, n_ctx_row):
    b, t, d = xall.shape
    tm = TOK_TILE
    nt = t // tm

    def tok(width):
        return pl.BlockSpec((None, tm, width), lambda j, i: (i, j, 0))

    def trn(height):
        return pl.BlockSpec((None, height, tm), lambda j, i: (i, 0, j))

    def const(arr):
        return pl.BlockSpec(arr.shape, lambda j, i: (0,) * arr.ndim)

    def tab_tok():
        return pl.BlockSpec((tm, HEAD_PAD), lambda j, i: (j, 0))

    def tab_trn(height):
        return pl.BlockSpec((height, tm), lambda j, i: (0, j))

    consts = [pw["gqt"], pw["wuqt"], pw["gkv"], pw["wuk"], pw["gkvt"], pw["wuvt"]]
    in_specs = [tok(d),
                pl.BlockSpec((None, 6, d), lambda j, i: (jnp.where(j == nt - 1, n_ctx_row, i), 0, 0)),
                const(pw["wtok"]), const(pw["wt"]),
                tab_trn(64), tab_trn(64), tab_tok(), tab_tok(),
                tab_trn(32), tab_trn(32), tab_tok(), tab_tok(),
                ] + [const(a) for a in consts]
    va_rows, vm_rows, vn_rows = N_HEADS * DA_VROWS, N_HEADS * MLA_VROWS, N_HEADS * NA_VROWS
    out_specs = [trn(512), tok(512), trn(va_rows), trn(512), tok(512), trn(vm_rows),
                 trn(256), tok(256), trn(vn_rows)]
    out_shape = [jax.ShapeDtypeStruct((b, 512, t), BF16), jax.ShapeDtypeStruct((b, t, 512), BF16),
                 jax.ShapeDtypeStruct((b, va_rows, t), BF16), jax.ShapeDtypeStruct((b, 512, t), BF16),
                 jax.ShapeDtypeStruct((b, t, 512), BF16), jax.ShapeDtypeStruct((b, vm_rows, t), BF16),
                 jax.ShapeDtypeStruct((b, 256, t), BF16), jax.ShapeDtypeStruct((b, t, 256), BF16),
                 jax.ShapeDtypeStruct((b, vn_rows, t), BF16)]
    return pl.pallas_call(
        _proj_kernel,
        grid=(nt, b),
        in_specs=in_specs,
        out_specs=out_specs,
        out_shape=out_shape,
        compiler_params=_cparams(("parallel", "parallel")),
        name="projection",
    )(xall, mod_l, pw["wtok"], pw["wt"],
      tabs["caq"], tabs["saq"], tabs["cak"], tabs["sak"], tabs["cmq"], tabs["smq"], tabs["cmk"], tabs["smk"],
      *consts)


def _fold_rows(op, x, rows):
    n = x.shape[0]
    if n > ROW_BLOCK:
        acc = x[0:ROW_BLOCK]
        for g in range(1, n // ROW_BLOCK):
            acc = op(acc, x[g * ROW_BLOCK:(g + 1) * ROW_BLOCK])
        x, n = acc, ROW_BLOCK
    while n > rows:
        n //= 2
        x = op(x[:n], x[n:])
    return x


def _attn_kernel(*refs, mode, tq, n_k, tk, lam_init, aliased_out):
    o_ref, m_ref, acc_ref, s_ref, cm_ref, p_ref, al_ref = refs[-7:]
    refs = refs[:-8] if aliased_out else refs[:-7]
    if mode == "da":
        qt_ref, k_ref, vt_ref, lam_ref, g_ref = refs
        kcols = ((0, 128), (0, 128))
        vdim, vrows = DA_VDIM, ((0, DA_VROWS), (0, DA_VROWS))
    else:
        qt_ref, k_ref, vt_ref = refs
        kcols = ((0, 128), (128, 256))
        vdim, vrows = MLA_VDIM, ((0, MLA_VROWS), (MLA_VROWS, 2 * MLA_VROWS))
    n_t, n_c = qt_ref.shape[1] // tq, n_k // tk
    total = n_t * n_c
    assert n_k % tk == 0 and tk % FINE == 0 and (total == 1 or n_c % 2 == 0)

    m_ref[...] = jnp.full(m_ref.shape, NEG_BIG, F32)
    acc_ref[...] = jnp.zeros(acc_ref.shape, F32)

    def query_maps(tile):
        col = tile * tq if isinstance(tile, int) else pl.multiple_of(tile * tq, tq)
        if mode == "da":
            qt = qt_ref[:, pl.ds(col, tq)]
            row = lax.broadcasted_iota(jnp.int32, qt.shape, 0)
            zero = jnp.zeros_like(qt)
            return (jnp.where(row < DA_DIM, qt, zero), jnp.where(row >= DA_DIM, qt, zero))
        return (qt_ref[0:128, pl.ds(col, tq)], qt_ref[128:256, pl.ds(col, tq)])

    def split(f):
        if isinstance(f, int):
            return f // n_c, (f % n_c) * tk
        tile = lax.div(f, jnp.int32(n_c))
        return tile, pl.multiple_of((f - tile * n_c) * tk, 128)

    def stage(f, par):
        static = isinstance(f, int)
        do_a = not static or f < total
        do_b = not static or 1 <= f <= total
        do_c = not static or 2 <= f <= total + 1
        if do_a:
            a_tile, a_start = split(f)
            qts = query_maps(a_tile)
        if do_b:
            _, b_start = split(f - 1)
        if do_c:
            _, c_start = split(f - 2)
        for i in range(2):
            if do_b:
                m_prev = jnp.where(b_start == 0, NEG_BIG, m_ref[i])
                m_new = jnp.maximum(m_prev, cm_ref[1 - par, i])
            cmax, pv = None, None
            for r in range(0, tk, FINE):
                if do_a:
                    st = _dot(k_ref[pl.ds(a_start + r, FINE), kcols[i][0]:kcols[i][1]], qts[i])
                    s_ref[par, i, r:r + FINE, :] = st
                    mx = _fold_rows(jnp.maximum, st, 8)
                    cmax = mx if cmax is None else jnp.maximum(cmax, mx)
                if do_b:
                    for rr in range(r, r + FINE, ROW_BLOCK):
                        x = s_ref[1 - par, i, rr:rr + ROW_BLOCK, :] - m_new
                        p_ref[1 - par, i, rr:rr + ROW_BLOCK, :] = jnp.exp2(x.astype(BF16))
                if do_c and r % 256 == 0:
                    w = min(256, tk - r)
                    d = _dot(vt_ref[vrows[i][0]:vrows[i][1], pl.ds(c_start + r, w)], p_ref[par, i, r:r + w, :])
                    pv = d if pv is None else pv + d
            if do_b:
                al_ref[1 - par, i] = jnp.exp2(m_prev - m_new)
                m_ref[i] = m_new
            if do_c:
                acc_ref[i] = al_ref[par, i] * acc_ref[i] + pv
            if do_a:
                cm_ref[par, i] = jnp.max(cmax, axis=0, keepdims=True)

    def finish_tile(tile):
        o0 = acc_ref[0, 0:vdim, :] / acc_ref[0, vdim:vdim + 1, :]
        o1 = acc_ref[1, 0:vdim, :] / acc_ref[1, vdim:vdim + 1, :]
        if mode == "da":
            lv = lam_ref[...]
            lam = (jnp.exp(jnp.sum(lv[0:1] * lv[1:2], axis=-1, keepdims=True))
                   - jnp.exp(jnp.sum(lv[2:3] * lv[3:4], axis=-1, keepdims=True)) + lam_init)
            o = o0 - lam * o1
            o = _rms(o, 0, SUBLN_EPS) * g_ref[...] * (1.0 - lam_init)
        else:
            o = jnp.concatenate([o0, o1], axis=0)
        row = tile * tq if isinstance(tile, int) else pl.multiple_of(tile * tq, tq)
        o_ref[pl.ds(row, tq), :] = o.T.astype(o_ref.dtype)

    def static_stage(f):
        stage(f, f % 2)
        if f >= 2 and (f - 2) % n_c == n_c - 1:
            finish_tile((f - 2) // n_c)

    for f in range(0, min(2, total + 2)):
        static_stage(f)

    def body(j, carry):
        f0 = 2 * j
        stage(f0, 0)
        stage(f0 + 1, 1)

        @pl.when(lax.rem(f0 - 1, jnp.int32(n_c)) == n_c - 1)
        def _():
            finish_tile(lax.div(f0 - 1, jnp.int32(n_c)))
        return carry

    if total > 2:
        lax.fori_loop(1, total // 2, body, 0)
    for f in range(max(2, total), total + 2):
        static_stage(f)


def _dense_attention(mode, qt, k, vt, *, n_q, tq, tk, q_blk0, k_blk0, n_k, out=None, out_rows=None,
                     extra=(), lam_init=0.0):
    b, _, t = qt.shape
    groups = N_HEADS if mode == "da" else N_HEADS // 2
    qw = 128 if mode == "da" else 256
    vblk = DA_VROWS if mode == "da" else 2 * MLA_VROWS
    acc_rows = DA_VROWS if mode == "da" else MLA_VROWS
    aliased = out is not None
    out_shape = jax.ShapeDtypeStruct(out.shape if aliased else (b, out_rows, 128 * groups), BF16)
    in_specs = [pl.BlockSpec((None, qw, n_q), lambda i, g: (i, g, q_blk0)),
                pl.BlockSpec((None, n_k, qw), lambda i, g: (i, k_blk0, g)),
                pl.BlockSpec((None, vblk, n_k), lambda i, g: (i, g, k_blk0))]
    args = [qt, k, vt]
    for a in extra:
        in_specs.append(pl.BlockSpec(a.shape, lambda i, g: (0, 0)))
        args.append(a)
    alias = {}
    if aliased:
        in_specs.append(pl.BlockSpec(memory_space=pl.ANY))
        alias = {len(args): 0}
        args.append(out)
    return pl.pallas_call(
        functools.partial(_attn_kernel, mode=mode, tq=tq, n_k=n_k, tk=tk, lam_init=lam_init,
                          aliased_out=aliased),
        grid=(b, groups),
        in_specs=in_specs,
        out_specs=pl.BlockSpec((None, n_q, 128), lambda i, g: (i, q_blk0, g)),
        out_shape=out_shape,
        scratch_shapes=[pltpu.VMEM((2, 1, tq), F32),
                        pltpu.VMEM((2, acc_rows, tq), F32), pltpu.VMEM((2, 2, tk, tq), F32),
                        pltpu.VMEM((2, 2, 1, tq), F32), pltpu.VMEM((2, 2, tk, tq), BF16),
                        pltpu.VMEM((2, 2, 1, tq), F32)],
        input_output_aliases=alias,
        compiler_params=_cparams(("parallel", "parallel")),
        name="attn_" + mode,
    )(*args)


def _na_kernel(*refs, rows, ctx_off, n_ctx, use_window):
    if use_window:
        qt_ref, k_ref, vt_ref, bias_ref, o_ref, s_ref, p_ref = refs
    else:
        qt_ref, k_ref, vt_ref, o_ref, s_ref, p_ref = refs
    qt = qt_ref[...]
    row = lax.broadcasted_iota(jnp.int32, qt.shape, 0)
    zero = jnp.zeros_like(qt)
    n_win = NA_WIN_ROWS * GRID_W if use_window else 0
    if use_window:
        i = pl.program_id(1)
        kr0 = jnp.clip(i * NA_Q_ROWS - NA_KH // 2, 0, rows - NA_WIN_ROWS)
        start = pl.multiple_of(kr0 * GRID_W, 128)
    maxima, outs = [], []
    for h in range(N_HEADS):
        qh = jnp.where((row >= h * NA_DIM) & (row < (h + 1) * NA_DIM), qt, zero)
        s_c = _dot(k_ref[ctx_off:ctx_off + n_ctx, :], qh)
        s_ref[h, n_win:n_win + n_ctx, :] = s_c
        cmax = _fold_rows(jnp.maximum, s_c, 8)
        if use_window:
            s_w = _dot(k_ref[pl.ds(start, n_win), :], qh) + bias_ref[h]
            s_ref[h, 0:n_win, :] = s_w
            cmax = jnp.maximum(cmax, _fold_rows(jnp.maximum, s_w, 8))
        maxima.append(jnp.max(cmax, axis=0, keepdims=True))
    for h in range(N_HEADS):
        for r in range(0, n_win + n_ctx, ROW_BLOCK):
            p_ref[h, r:r + ROW_BLOCK, :] = jnp.exp2((s_ref[h, r:r + ROW_BLOCK, :] - maxima[h]).astype(BF16))
    for h in range(N_HEADS):
        vrows = slice(h * NA_VROWS, (h + 1) * NA_VROWS)
        o = _dot(vt_ref[vrows, ctx_off:ctx_off + n_ctx], p_ref[h, n_win:n_win + n_ctx, :])
        if use_window:
            o = o + _dot(vt_ref[vrows, pl.ds(start, n_win)], p_ref[h, 0:n_win, :])
        outs.append(o[0:NA_DIM] / o[NA_DIM:NA_DIM + 1])
    o_ref[...] = jnp.concatenate(outs, axis=0).T.astype(o_ref.dtype)


def _na_bias_index(rows):
    nb = rows // NA_Q_ROWS
    assert rows % NA_Q_ROWS == 0 and nb >= 3 and rows >= NA_WIN_ROWS and (rows - NA_WIN_ROWS) % 2 == 0

    def block(i):
        r = i * NA_Q_ROWS + np.arange(NA_Q_ROWS)
        kr = int(np.clip(i * NA_Q_ROWS - NA_KH // 2, 0, rows - NA_WIN_ROWS)) + np.arange(NA_WIN_ROWS)
        rs = np.clip(r - NA_KH // 2, 0, rows - NA_KH)
        valid = (kr[None, :] >= rs[:, None]) & (kr[None, :] < rs[:, None] + NA_KH)
        dy = np.clip(kr[None, :] - r[:, None] + NA_KH - 1, 0, 2 * NA_KH - 2)
        return valid, dy

    blocks = [block(i) for i in range(nb)]
    for i in range(2, nb - 1):
        assert all(np.array_equal(a, b_) for a, b_ in zip(blocks[1], blocks[i]))
    cls = [blocks[0], blocks[1], blocks[nb - 1]]
    rvalid = np.stack([c[0] for c in cls]).reshape(-1)
    dy = np.stack([c[1] for c in cls]).reshape(-1)
    rsel = (dy[:, None] == np.arange(2 * NA_KH - 1)[None, :]).astype(np.float32)

    c = np.arange(GRID_W)
    cs = np.clip(c - NA_KW // 2, 0, GRID_W - NA_KW)
    cvalid = ((c[None, :] >= cs[:, None]) & (c[None, :] < cs[:, None] + NA_KW)).reshape(-1)
    dx = np.clip(c[None, :] - c[:, None] + NA_KW - 1, 0, 2 * NA_KW - 2).reshape(-1)
    csel = (np.arange(2 * NA_KW - 1)[:, None] == dx[None, :]).astype(np.float32)
    return rsel, rvalid.astype(np.float32)[:, None], csel, cvalid.astype(np.float32)[None, :]


def _na_bias_kernel(rpb_ref, rsel_ref, rmask_ref, csel_ref, cmask_ref, o_ref):
    cols = _dot_exact(rpb_ref[...], csel_ref[...])
    vals = _dot_exact(rsel_ref[...], cols)
    valid = (rmask_ref[...] * cmask_ref[...]) > 0.5
    o_ref[...] = jnp.where(valid, vals * LOG2E, NEG_BIG)


def _na_bias(rpb, rows):
    rsel, rmask, csel, cmask = (jnp.asarray(a) for a in _na_bias_index(rows))
    nh = rpb.shape[0]
    n_r, n_c = rsel.shape[0], csel.shape[1]
    full = lambda a: pl.BlockSpec(a.shape, lambda h: (0, 0))
    out = pl.pallas_call(
        _na_bias_kernel,
        grid=(nh,),
        in_specs=[pl.BlockSpec((None,) + rpb.shape[1:], lambda h: (h, 0, 0)),
                  full(rsel), full(rmask), full(csel), full(cmask)],
        out_specs=pl.BlockSpec((None, n_r, n_c), lambda h: (h, 0, 0)),
        out_shape=jax.ShapeDtypeStruct((nh, n_r, n_c), F32),
        compiler_params=_cparams(("parallel",)),
        name="na_bias",
    )(rpb.astype(F32), rsel, rmask, csel, cmask)
    out = out.reshape(nh, 3, NA_Q_ROWS, NA_WIN_ROWS, GRID_W, GRID_W)
    out = jnp.transpose(out, (1, 0, 3, 5, 2, 4))
    return out.reshape(3, nh, NA_WIN_ROWS * GRID_W, NA_Q_ROWS * GRID_W)


def _neighbourhood_attention(nqt, nk, nvt, bias, seq, out_rows):
    b, t, w = nk.shape
    tq = NA_Q_ROWS * GRID_W
    nb = seq // tq
    rows = seq // GRID_W
    n_keys = NA_WIN_ROWS * GRID_W + t - seq
    return pl.pallas_call(
        functools.partial(_na_kernel, rows=rows, ctx_off=seq, n_ctx=t - seq, use_window=True),
        grid=(b, nb),
        in_specs=[pl.BlockSpec((None, w, tq), lambda i, j: (i, 0, j)),
                  pl.BlockSpec((None, t, w), lambda i, j: (i, 0, 0)),
                  pl.BlockSpec((None, nvt.shape[1], t), lambda i, j: (i, 0, 0)),
                  pl.BlockSpec((None,) + bias.shape[1:],
                               lambda i, j: (jnp.where(j == 0, 0, jnp.where(j == nb - 1, 2, 1)), 0, 0, 0))],
        out_specs=pl.BlockSpec((None, tq, w), lambda i, j: (i, j, 0)),
        out_shape=jax.ShapeDtypeStruct((b, out_rows, w), BF16),
        scratch_shapes=[pltpu.VMEM((N_HEADS, n_keys, tq), F32), pltpu.VMEM((N_HEADS, n_keys, tq), BF16)],
        compiler_params=_cparams(("parallel", "parallel")),
        name="attn_na",
    )(nqt, nk, nvt, bias)


def _neighbourhood_attention_ctx(nqt, nk, nvt, out, seq):
    b, t, w = nk.shape
    n_ctx = t - seq
    blk = seq // n_ctx
    kern = lambda q, k, v, _, o, *scratch, **kw: _na_kernel(q, k, v, o, *scratch, **kw)
    return pl.pallas_call(
        functools.partial(kern, rows=0, ctx_off=0, n_ctx=n_ctx, use_window=False),
        grid=(b,),
        in_specs=[pl.BlockSpec((None, w, n_ctx), lambda i: (i, 0, blk)),
                  pl.BlockSpec((None, n_ctx, w), lambda i: (i, blk, 0)),
                  pl.BlockSpec((None, nvt.shape[1], n_ctx), lambda i: (i, 0, blk)),
                  pl.BlockSpec(memory_space=pl.ANY)],
        out_specs=pl.BlockSpec((None, n_ctx, w), lambda i: (i, blk, 0)),
        out_shape=jax.ShapeDtypeStruct(out.shape, out.dtype),
        scratch_shapes=[pltpu.VMEM((N_HEADS, n_ctx, n_ctx), F32), pltpu.VMEM((N_HEADS, n_ctx, n_ctx), BF16)],
        input_output_aliases={3: 0},
        compiler_params=_cparams(("parallel",)),
        name="attn_na_ctx",
    )(nqt, nk, nvt, out)


def _layer_norm(z, g, b):
    mu = jnp.mean(z, axis=-1, keepdims=True)
    zc = z - mu
    var = jnp.mean(jnp.square(zc), axis=-1, keepdims=True)
    return zc * lax.rsqrt(var + LN_EPS) * g + b


def _ffn_kernel(x_ref, ya_ref, yb_ref, yc_ref, mod_ref, wo_ref, w1_ref, w2_ref, ln_ref, o_ref, *, alpha):
    x = x_ref[...]
    a = (_dot(ya_ref[...], wo_ref[0:512, :]) + _dot(yb_ref[...], wo_ref[512:768, :])
         + _dot(yc_ref[...], wo_ref[768:1024, :]))
    x1 = _layer_norm(alpha * x + mod_ref[2:3, :] * a, ln_ref[0:1, :], ln_ref[1:2, :])
    h2 = (x1 * (1.0 + mod_ref[4:5, :]) + mod_ref[3:4, :]).astype(BF16)
    u = jnp.maximum(_dot(h2, w1_ref[...]), 0.0)
    f = _dot(jnp.square(u).astype(BF16), w2_ref[...])
    o_ref[...] = _layer_norm(alpha * x1 + mod_ref[5:6, :] * f, ln_ref[2:3, :], ln_ref[3:4, :])


def _out_ffn(xall, ya, yb, yc, mod_l, wo, w1, w2, ln, n_tok, n_ctx_row, alpha):
    b, t, d = xall.shape
    tm = TOK_TILE
    nt_all = t // tm
    nt = n_tok // tm

    def tok(width):
        return pl.BlockSpec((None, tm, width), lambda i, j: (i, j, 0))

    def const(arr):
        return pl.BlockSpec(arr.shape, lambda i, j: (0,) * arr.ndim, pipeline_mode=pl.Buffered(1))

    return pl.pallas_call(
        functools.partial(_ffn_kernel, alpha=alpha),
        grid=(b, nt),
        in_specs=[tok(d), tok(512), tok(256), tok(256),
                  pl.BlockSpec((None, 6, d), lambda i, j: (jnp.where(j == nt_all - 1, n_ctx_row, i), 0, 0)),
                  const(wo), const(w1), const(w2), const(ln)],
        out_specs=tok(d),
        out_shape=jax.ShapeDtypeStruct((b, n_tok, d), F32),
        compiler_params=_cparams(("parallel", "parallel")),
        name="out_ffn",
    )(xall, ya, yb, yc, mod_l, wo, w1, w2, ln)


def _rope_tables(seq, n_ctx):
    t = jnp.arange(seq, dtype=jnp.int32)
    row = (t // GRID_W).astype(F32)
    col = (t % GRID_W).astype(F32)

    def axial(n):
        half = n // 2
        inv = ROPE_BASE ** (-2.0 * jnp.arange(half, dtype=F32) / n)
        ar, ac = row[:, None] * inv[None, :], col[:, None] * inv[None, :]
        cos = jnp.concatenate([jnp.cos(ar), jnp.cos(ar), jnp.cos(ac), jnp.cos(ac)], axis=-1)
        sin = jnp.concatenate([-jnp.sin(ar), jnp.sin(ar), -jnp.sin(ac), jnp.sin(ac)], axis=-1)
        cos = jnp.concatenate([cos, jnp.ones((n_ctx, 2 * n), F32)], axis=0)
        sin = jnp.concatenate([sin, jnp.zeros((n_ctx, 2 * n), F32)], axis=0)
        return cos, sin

    cos_a, sin_a = axial(DA_DIM // 2)
    cos_m, sin_m = axial(MLA_ROPE // 2)
    sa = DA_DIM ** -0.5 * LOG2E
    sm = (MLA_NOPE + MLA_ROPE) ** -0.5 * LOG2E
    tt = seq + n_ctx
    ones = jnp.ones((tt, MLA_NOPE), F32)
    zpad = jnp.zeros((tt, MLA_ZPAD), F32)
    return dict(
        caq=cos_a.T * sa, saq=sin_a.T * sa,
        cak=jnp.tile(cos_a, (1, 2)), sak=jnp.tile(sin_a, (1, 2)),
        cmq=cos_m.T * sm, smq=sin_m.T * sm,
        cmk=jnp.concatenate([ones, cos_m, zpad], axis=-1),
        smk=jnp.concatenate([0.0 * ones, sin_m, zpad], axis=-1))


def _layer_weights(w_in, gq, w_uq, gkv, w_ukv):
    d = w_in.shape[0]
    o = np.cumsum([0, 512, 512, 512, MLA_Q_RANK, MLA_KV_RANK, MLA_ROPE, 256, 256, 256])
    col = lambda k: w_in[:, int(o[k]):int(o[k + 1])]
    aq, ak, av, cq, ckv, kr, nq, nk, nv = (col(k) for k in range(9))
    kr_placed = jnp.concatenate([jnp.zeros((d, MLA_NOPE), w_in.dtype), kr,
                                 jnp.zeros((d, MLA_ZPAD), w_in.dtype)], axis=1)
    wtok = jnp.concatenate([ak] + [kr_placed] * N_HEADS + [ckv, nk], axis=1).astype(BF16)
    wt = jnp.concatenate([aq, av, cq, ckv, nq, nv], axis=1).T.astype(BF16)
    uq = w_uq.reshape(MLA_Q_RANK, N_HEADS, MLA_NOPE + MLA_ROPE)
    uq = jnp.pad(uq, ((0, 0), (0, 0), (0, MLA_ZPAD)))
    ukv = w_ukv.reshape(MLA_KV_RANK, N_HEADS, MLA_NOPE + MLA_VDIM)
    uk = jnp.pad(ukv[:, :, :MLA_NOPE], ((0, 0), (0, 0), (0, HEAD_PAD - MLA_NOPE)))
    return dict(
        wtok=wtok, wt=wt,
        gqt=gq.reshape(-1, 1).astype(F32),
        wuqt=uq.reshape(MLA_Q_RANK, N_HEADS * HEAD_PAD).T.astype(BF16),
        gkv=gkv.reshape(1, -1).astype(F32), gkvt=gkv.reshape(-1, 1).astype(F32),
        wuk=uk.reshape(MLA_KV_RANK, N_HEADS * HEAD_PAD).astype(BF16),
        wuvt=ukv[:, :, MLA_NOPE:].reshape(MLA_KV_RANK, N_HEADS * MLA_VDIM).T.astype(BF16))


def _pick_key_chunk(n_k, tk_max):
    for n_c in range(2, n_k // FINE + 1, 2):
        if n_k % (n_c * FINE) == 0 and n_k // n_c <= tk_max:
            return n_k // n_c
    raise ValueError(f"no even chunking of {n_k} keys")


def kernel(x, c, ctx, c_ctx, w_mod, b_mod, w_in, da_lam_q1, da_lam_k1, da_lam_q2, da_lam_k2, da_subln_g,
           mla_q_norm_g, mla_w_uq, mla_kv_norm_g, mla_w_ukv, na_rpb, w_out, ln1_g, ln1_b, w_ff1, w_ff2,
           ln2_g, ln2_b):
    b, seq, d = x.shape
    n_ctx = ctx.shape[1]
    depth = w_mod.shape[0]
    assert seq % TOK_TILE == 0 and n_ctx == TOK_TILE and b < 8 and seq % (NA_Q_ROWS * GRID_W) == 0
    alpha = (2.0 * depth) ** 0.25
    tq = ATTN_TQ
    tk_da, tk_mla = (_pick_key_chunk(seq + n_ctx, m) for m in (DA_TK_MAX, MLA_TK_MAX))
    assert seq % tq == 0 and n_ctx % FINE == 0

    cs = jnp.zeros((8, d), F32).at[:b].set(c).at[b].set(c_ctx)
    mod = _modulation(cs, w_mod, b_mod).reshape(depth, 8, 6, d)
    tabs = _rope_tables(seq, n_ctx)
    xall = jnp.concatenate([x, ctx], axis=1)
    ctx_blk = seq // n_ctx

    for l in range(depth):
        last = l == depth - 1
        lam_init = 0.8 - 0.6 * math.exp(-0.3 * l)
        pw = _layer_weights(w_in[l], mla_q_norm_g[l], mla_w_uq[l], mla_kv_norm_g[l], mla_w_ukv[l])
        qat, ka, vat, qmt, km, vmt, nqt, nk, nvt = _projection(xall, mod[l], pw, tabs, b)

        lam_vec = jnp.stack([da_lam_q1[l], da_lam_k1[l], da_lam_q2[l], da_lam_k2[l]]).astype(F32)
        g_sub = da_subln_g[l].reshape(-1, 1).astype(F32)
        n_out = seq if last else seq + n_ctx
        ya = _dense_attention("da", qat, ka, vat, n_q=seq, tq=tq, tk=tk_da, q_blk0=0, k_blk0=0, n_k=seq + n_ctx,
                              out_rows=n_out, extra=(lam_vec, g_sub), lam_init=lam_init)
        yb = _dense_attention("mla", qmt, km, vmt, n_q=seq, tq=tq, tk=tk_mla, q_blk0=0, k_blk0=0, n_k=seq + n_ctx,
                              out_rows=n_out)
        yc = _neighbourhood_attention(nqt, nk, nvt, _na_bias(na_rpb[l], seq // GRID_W), seq, n_out)
        if not last:
            ya = _dense_attention("da", qat, ka, vat, n_q=n_ctx, tq=n_ctx, tk=n_ctx, q_blk0=ctx_blk,
                                  k_blk0=ctx_blk, n_k=n_ctx, out=ya, extra=(lam_vec, g_sub), lam_init=lam_init)
            yb = _dense_attention("mla", qmt, km, vmt, n_q=n_ctx, tq=n_ctx, tk=n_ctx, q_blk0=ctx_blk,
                                  k_blk0=ctx_blk, n_k=n_ctx, out=yb)
            yc = _neighbourhood_attention_ctx(nqt, nk, nvt, yc, seq)

        ln = jnp.stack([ln1_g[l], ln1_b[l], ln2_g[l], ln2_b[l]]).astype(F32)
        xall = _out_ffn(xall, ya, yb, yc, mod[l], w_out[l].astype(BF16), w_ff1[l].astype(BF16),
                        w_ff2[l].astype(BF16), ln, n_out, b, alpha)
    return xall
```

```python
import functools
import math

import numpy as np
import jax
import jax.numpy as jnp
from jax import lax
from jax.experimental import pallas as pl
from jax.experimental.pallas import tpu as pltpu

F32 = jnp.float32
BF16 = jnp.bfloat16

GRID_W = 64
ROPE_BASE = 10000.0
LN_EPS = 1e-6
RMS_EPS = 1e-6
SUBLN_EPS = 1e-5
N_HEADS = 4
DA_DIM = 64
DA_VDIM = 128
MLA_Q_RANK = 256
MLA_KV_RANK = 128
MLA_NOPE = 64
MLA_ROPE = 32
MLA_VDIM = 64
NA_DIM = 64
NA_KH = 8
NA_KW = 16
HEAD_PAD = 128
MLA_ZPAD = HEAD_PAD - MLA_NOPE - MLA_ROPE
ONES_ROWS = 16
DA_VROWS = DA_VDIM + ONES_ROWS
MLA_VROWS = MLA_VDIM + ONES_ROWS
NA_VROWS = NA_DIM + ONES_ROWS

LOG2E = math.log2(math.e)
NEG_BIG = -1e30

TOK_TILE = 256
ATTN_TQ = 256
DA_TK_MAX = 4224
MLA_TK_MAX = 4224
FINE = 128
ROW_BLOCK = 64
NA_Q_ROWS = 4
NA_WIN_ROWS = NA_Q_ROWS + NA_KH
VMEM_LIMIT = 56 * 1024 * 1024


def _cparams(sem, flags=None):
    return pltpu.CompilerParams(dimension_semantics=sem, vmem_limit_bytes=VMEM_LIMIT, flags=flags)


def _dot(a, b):
    return jnp.dot(a, b, preferred_element_type=F32)


def _dot_exact(a, b):
    return jnp.dot(a, b, preferred_element_type=F32, precision=lax.Precision.HIGHEST)


def _mod_kernel(c_ref, w_ref, b_ref, o_ref):
    c = c_ref[...]
    s = c / (1.0 + jnp.exp(-c))
    o_ref[...] = _dot_exact(s, w_ref[...]) + b_ref[...]


def _modulation(cs, w_mod, b_mod):
    depth, d, n = w_mod.shape
    nb = 1024
    return pl.pallas_call(
        _mod_kernel,
        grid=(depth, n // nb),
        in_specs=[pl.BlockSpec((8, d), lambda l, j: (0, 0)),
                  pl.BlockSpec((None, d, nb), lambda l, j: (l, 0, j)),
                  pl.BlockSpec((None, 1, nb), lambda l, j: (l, 0, j))],
        out_specs=pl.BlockSpec((None, 8, nb), lambda l, j: (l, 0, j)),
        out_shape=jax.ShapeDtypeStruct((depth, 8, n), F32),
        compiler_params=_cparams(("arbitrary", "arbitrary")),
        name="modulation",
    )(cs, w_mod, b_mod.reshape(depth, 1, n))


def _rms(x, axis, eps):
    return x * lax.rsqrt(jnp.mean(jnp.square(x), axis=axis, keepdims=True) + eps)


def _rope_lanes(x, lane, half, cos, sin):
    partner = jnp.where((lane & half) == 0, pltpu.roll(x, HEAD_PAD - half, 1), pltpu.roll(x, half, 1))
    return x * cos + partner * sin


def _rope_rows(x, half, cos, sin):
    partner = jnp.concatenate([x[half:2 * half], x[0:half], x[3 * half:4 * half], x[2 * half:3 * half]], axis=0)
    return x * cos + partner * sin


def _proj_kernel(x_ref, mod_ref, wtok_ref, wt_ref,
                 caq_ref, saq_ref, cak_ref, sak_ref, cmq_ref, smq_ref, cmk_ref, smk_ref,
                 gqt_ref, wuqt_ref, gkv_ref, wuk_ref, gkvt_ref, wuvt_ref,
                 qat_ref, ka_ref, vat_ref, qmt_ref, km_ref, vmt_ref, nqt_ref, nk_ref, nvt_ref):
    x = x_ref[...]
    h = (x * (1.0 + mod_ref[1:2, :]) + mod_ref[0:1, :]).astype(BF16)
    p = _dot(h, wtok_ref[...])
    pt = lax.dot_general(wt_ref[...], h, (((1,), (1,)), ((), ())),
                         preferred_element_type=F32)
    tm = x.shape[0]
    lane = lax.broadcasted_iota(jnp.int32, (tm, HEAD_PAD), 1)

    caq, saq = caq_ref[...], saq_ref[...]
    for g in range(2 * N_HEADS):
        rows = slice(g * DA_DIM, (g + 1) * DA_DIM)
        qat_ref[rows, :] = _rope_rows(pt[rows, :], DA_DIM // 4, caq, saq).astype(BF16)
    cak, sak = cak_ref[...], sak_ref[...]
    for g in range(N_HEADS):
        cols = slice(g * HEAD_PAD, (g + 1) * HEAD_PAD)
        ka_ref[:, cols] = _rope_lanes(p[:, cols], lane, DA_DIM // 4, cak, sak).astype(BF16)
    ones = jnp.ones((ONES_ROWS, tm), BF16)
    for g in range(N_HEADS):
        r0 = g * DA_VROWS
        vat_ref[r0:r0 + DA_VDIM, :] = pt[512 + g * DA_VDIM:512 + (g + 1) * DA_VDIM, :].astype(BF16)
        vat_ref[r0 + DA_VDIM:r0 + DA_VROWS, :] = ones

    cqnt = (_rms(pt[1024:1280, :], 0, RMS_EPS) * gqt_ref[...]).astype(BF16)
    qmt = _dot(wuqt_ref[...], cqnt)
    cmq, smq = cmq_ref[...], smq_ref[...]
    scale_m = (MLA_NOPE + MLA_ROPE) ** -0.5 * LOG2E
    for g in range(N_HEADS):
        r0 = g * HEAD_PAD
        qmt_ref[r0:r0 + MLA_NOPE, :] = (qmt[r0:r0 + MLA_NOPE, :] * scale_m).astype(BF16)
        qmt_ref[r0 + MLA_NOPE:r0 + MLA_NOPE + MLA_ROPE, :] = _rope_rows(
            qmt[r0 + MLA_NOPE:r0 + MLA_NOPE + MLA_ROPE, :], MLA_ROPE // 4, cmq, smq).astype(BF16)
        qmt_ref[r0 + MLA_NOPE + MLA_ROPE:r0 + HEAD_PAD, :] = jnp.zeros((MLA_ZPAD, tm), BF16)
    ckvn = (_rms(p[:, 1024:1152], -1, RMS_EPS) * gkv_ref[...]).astype(BF16)
    km = _dot(ckvn, wuk_ref[...]) + p[:, 512:1024]
    cmk, smk = cmk_ref[...], smk_ref[...]
    for g in range(N_HEADS):
        cols = slice(g * HEAD_PAD, (g + 1) * HEAD_PAD)
        km_ref[:, cols] = _rope_lanes(km[:, cols], lane, MLA_ROPE // 4, cmk, smk).astype(BF16)
    ckvnt = (_rms(pt[1280:1408, :], 0, RMS_EPS) * gkvt_ref[...]).astype(BF16)
    vmt = _dot(wuvt_ref[...], ckvnt)
    for g in range(N_HEADS):
        r0 = g * MLA_VROWS
        vmt_ref[r0:r0 + MLA_VDIM, :] = vmt[g * MLA_VDIM:(g + 1) * MLA_VDIM, :].astype(BF16)
        vmt_ref[r0 + MLA_VDIM:r0 + MLA_VROWS, :] = ones

    nqt_ref[...] = (pt[1408:1664, :] * (NA_DIM ** -0.5 * LOG2E)).astype(BF16)
    nk_ref[...] = p[:, 1152:1408].astype(BF16)
    for g in range(N_HEADS):
        r0 = g * NA_VROWS
        nvt_ref[r0:r0 + NA_DIM, :] = pt[1664 + g * NA_DIM:1664 + (g + 1) * NA_DIM, :].astype(BF16)
        nvt_ref[r0 + NA_DIM:r0 + NA_VROWS, :] = ones


def _projection(xall, mod_l, pw, tabs, n_ctx_row):
    b, t, d = xall.shape
    tm = TOK_TILE
    nt = t // tm

    def tok(width):
        return pl.BlockSpec((None, tm, width), lambda j, i: (i, j, 0))

    def trn(height):
        return pl.BlockSpec((None, height, tm), lambda j, i: (i, 0, j))

    def const(arr):
        return pl.BlockSpec(arr.shape, lambda j, i: (0,) * arr.ndim)

    def tab_tok():
        return pl.BlockSpec((tm, HEAD_PAD), lambda j, i: (j, 0))

    def tab_trn(height):
        return pl.BlockSpec((height, tm), lambda j, i: (0, j))

    consts = [pw["gqt"], pw["wuqt"], pw["gkv"], pw["wuk"], pw["gkvt"], pw["wuvt"]]
    in_specs = [tok(d),
                pl.BlockSpec((None, 6, d), lambda j, i: (jnp.where(j == nt - 1, n_ctx_row, i), 0, 0)),
                const(pw["wtok"]), const(pw["wt"]),
                tab_trn(64), tab_trn(64), tab_tok(), tab_tok(),
                tab_trn(32), tab_trn(32), tab_tok(), tab_tok(),
                ] + [const(a) for a in consts]
    va_rows, vm_rows, vn_rows = N_HEADS * DA_VROWS, N_HEADS * MLA_VROWS, N_HEADS * NA_VROWS
    out_specs = [trn(512), tok(512), trn(va_rows), trn(512), tok(512), trn(vm_rows),
                 trn(256), tok(256), trn(vn_rows)]
    out_shape = [jax.ShapeDtypeStruct((b, 512, t), BF16), jax.ShapeDtypeStruct((b, t, 512), BF16),
                 jax.ShapeDtypeStruct((b, va_rows, t), BF16), jax.ShapeDtypeStruct((b, 512, t), BF16),
                 jax.ShapeDtypeStruct((b, t, 512), BF16), jax.ShapeDtypeStruct((b, vm_rows, t), BF16),
                 jax.ShapeDtypeStruct((b, 256, t), BF16), jax.ShapeDtypeStruct((b, t, 256), BF16),
                 jax.ShapeDtypeStruct((b, vn_rows, t), BF16)]
    return pl.pallas_call(
        _proj_kernel,
        grid=(nt, b),
        in_specs=in_specs,
        out_specs=out_specs,
        out_shape=out_shape,
        compiler_params=_cparams(("parallel", "parallel")),
        name="projection",
    )(xall, mod_l, pw["wtok"], pw["wt"],
      tabs["caq"], tabs["saq"], tabs["cak"], tabs["sak"], tabs["cmq"], tabs["smq"], tabs["cmk"], tabs["smk"],
      *consts)


def _fold_rows(op, x, rows):
    n = x.shape[0]
    if n > ROW_BLOCK:
        acc = x[0:ROW_BLOCK]
        for g in range(1, n // ROW_BLOCK):
            acc = op(acc, x[g * ROW_BLOCK:(g + 1) * ROW_BLOCK])
        x, n = acc, ROW_BLOCK
    while n > rows:
        n //= 2
        x = op(x[:n], x[n:])
    return x


def _attn_kernel(*refs, mode, tq, n_k, tk, lam_init, aliased_out):
    o_ref, m_ref, acc_ref, s_ref, cm_ref, p_ref, al_ref = refs[-7:]
    refs = refs[:-8] if aliased_out else refs[:-7]
    if mode == "da":
        qt_ref, k_ref, vt_ref, lam_ref, g_ref = refs
        kcols = ((0, 128), (0, 128))
        vdim, vrows = DA_VDIM, ((0, DA_VROWS), (0, DA_VROWS))
    else:
        qt_ref, k_ref, vt_ref = refs
        kcols = ((0, 128), (128, 256))
        vdim, vrows = MLA_VDIM, ((0, MLA_VROWS), (MLA_VROWS, 2 * MLA_VROWS))
    n_t, n_c = qt_ref.shape[1] // tq, n_k // tk
    total = n_t * n_c
    assert n_k % tk == 0 and tk % FINE == 0 and (total == 1 or n_c % 2 == 0)

    m_ref[...] = jnp.full(m_ref.shape, NEG_BIG, F32)
    acc_ref[...] = jnp.zeros(acc_ref.shape, F32)

    def query_maps(tile):
        col = tile * tq if isinstance(tile, int) else pl.multiple_of(tile * tq, tq)
        if mode == "da":
            qt = qt_ref[:, pl.ds(col, tq)]
            row = lax.broadcasted_iota(jnp.int32, qt.shape, 0)
            zero = jnp.zeros_like(qt)
            return (jnp.where(row < DA_DIM, qt, zero), jnp.where(row >= DA_DIM, qt, zero))
        return (qt_ref[0:128, pl.ds(col, tq)], qt_ref[128:256, pl.ds(col, tq)])

    def split(f):
        if isinstance(f, int):
            return f // n_c, (f % n_c) * tk
        tile = lax.div(f, jnp.int32(n_c))
        return tile, pl.multiple_of((f - tile * n_c) * tk, 128)

    def stage(f, par):
        static = isinstance(f, int)
        do_a = not static or f < total
        do_b = not static or 1 <= f <= total
        do_c = not static or 2 <= f <= total + 1
        if do_a:
            a_tile, a_start = split(f)
            qts = query_maps(a_tile)
        if do_b:
            _, b_start = split(f - 1)
        if do_c:
            _, c_start = split(f - 2)
        for i in range(2):
            if do_b:
                m_prev = jnp.where(b_start == 0, NEG_BIG, m_ref[i])
                m_new = jnp.maximum(m_prev, cm_ref[1 - par, i])
            cmax, pv = None, None
            for r in range(0, tk, FINE):
                if do_a:
                    st = _dot(k_ref[pl.ds(a_start + r, FINE), kcols[i][0]:kcols[i][1]], qts[i])
                    s_ref[par, i, r:r + FINE, :] = st
                    mx = _fold_rows(jnp.maximum, st, 8)
                    cmax = mx if cmax is None else jnp.maximum(cmax, mx)
                if do_b:
                    for rr in range(r, r + FINE, ROW_BLOCK):
                        x = s_ref[1 - par, i, rr:rr + ROW_BLOCK, :] - m_new
                        p_ref[1 - par, i, rr:rr + ROW_BLOCK, :] = jnp.exp2(x.astype(BF16))
                if do_c and r % 256 == 0:
                    w = min(256, tk - r)
                    d = _dot(vt_ref[vrows[i][0]:vrows[i][1], pl.ds(c_start + r, w)], p_ref[par, i, r:r + w, :])
                    pv = d if pv is None else pv + d
            if do_b:
                al_ref[1 - par, i] = jnp.exp2(m_prev - m_new)
                m_ref[i] = m_new
            if do_c:
                acc_ref[i] = al_ref[par, i] * acc_ref[i] + pv
            if do_a:
                cm_ref[par, i] = jnp.max(cmax, axis=0, keepdims=True)

    def finish_tile(tile):
        o0 = acc_ref[0, 0:vdim, :] / acc_ref[0, vdim:vdim + 1, :]
        o1 = acc_ref[1, 0:vdim, :] / acc_ref[1, vdim:vdim + 1, :]
        if mode == "da":
            lv = lam_ref[...]
            lam = (jnp.exp(jnp.sum(lv[0:1] * lv[1:2], axis=-1, keepdims=True))
                   - jnp.exp(jnp.sum(lv[2:3] * lv[3:4], axis=-1, keepdims=True)) + lam_init)
            o = o0 - lam * o1
            o = _rms(o, 0, SUBLN_EPS) * g_ref[...] * (1.0 - lam_init)
        else:
            o = jnp.concatenate([o0, o1], axis=0)
        row = tile * tq if isinstance(tile, int) else pl.multiple_of(tile * tq, tq)
        o_ref[pl.ds(row, tq), :] = o.T.astype(o_ref.dtype)

    def static_stage(f):
        stage(f, f % 2)
        if f >= 2 and (f - 2) % n_c == n_c - 1:
            finish_tile((f - 2) // n_c)

    for f in range(0, min(2, total + 2)):
        static_stage(f)

    def body(j, carry):
        f0 = 2 * j
        stage(f0, 0)
        stage(f0 + 1, 1)

        @pl.when(lax.rem(f0 - 1, jnp.int32(n_c)) == n_c - 1)
        def _():
            finish_tile(lax.div(f0 - 1, jnp.int32(n_c)))
        return carry

    if total > 2:
        lax.fori_loop(1, total // 2, body, 0)
    for f in range(max(2, total), total + 2):
        static_stage(f)


def _dense_attention(mode, qt, k, vt, *, n_q, tq, tk, q_blk0, k_blk0, n_k, out=None, out_rows=None,
                     extra=(), lam_init=0.0):
    b, _, t = qt.shape
    groups = N_HEADS if mode == "da" else N_HEADS // 2
    qw = 128 if mode == "da" else 256
    vblk = DA_VROWS if mode == "da" else 2 * MLA_VROWS
    acc_rows = DA_VROWS if mode == "da" else MLA_VROWS
    aliased = out is not None
    out_shape = jax.ShapeDtypeStruct(out.shape if aliased else (b, out_rows, 128 * groups), BF16)
    in_specs = [pl.BlockSpec((None, qw, n_q), lambda i, g: (i, g, q_blk0)),
                pl.BlockSpec((None, n_k, qw), lambda i, g: (i, k_blk0, g)),
                pl.BlockSpec((None, vblk, n_k), lambda i, g: (i, g, k_blk0))]
    args = [qt, k, vt]
    for a in extra:
        in_specs.append(pl.BlockSpec(a.shape, lambda i, g: (0, 0)))
        args.append(a)
    alias = {}
    if aliased:
        in_specs.append(pl.BlockSpec(memory_space=pl.ANY))
        alias = {len(args): 0}
        args.append(out)
    return pl.pallas_call(
        functools.partial(_attn_kernel, mode=mode, tq=tq, n_k=n_k, tk=tk, lam_init=lam_init,
                          aliased_out=aliased),
        grid=(b, groups),
        in_specs=in_specs,
        out_specs=pl.BlockSpec((None, n_q, 128), lambda i, g: (i, q_blk0, g)),
        out_shape=out_shape,
        scratch_shapes=[pltpu.VMEM((2, 1, tq), F32),
                        pltpu.VMEM((2, acc_rows, tq), F32), pltpu.VMEM((2, 2, tk, tq), F32),
                        pltpu.VMEM((2, 2, 1, tq), F32), pltpu.VMEM((2, 2, tk, tq), BF16),
                        pltpu.VMEM((2, 2, 1, tq), F32)],
        input_output_aliases=alias,
        compiler_params=_cparams(("parallel", "parallel")),
        name="attn_" + mode,
    )(*args)


def _na_kernel(*refs, rows, ctx_off, n_ctx, use_window):
    if use_window:
        qt_ref, k_ref, vt_ref, bias_ref, o_ref, s_ref, p_ref = refs
    else:
        qt_ref, k_ref, vt_ref, o_ref, s_ref, p_ref = refs
    qt = qt_ref[...]
    row = lax.broadcasted_iota(jnp.int32, qt.shape, 0)
    zero = jnp.zeros_like(qt)
    n_win = NA_WIN_ROWS * GRID_W if use_window else 0
    if use_window:
        i = pl.program_id(1)
        kr0 = jnp.clip(i * NA_Q_ROWS - NA_KH // 2, 0, rows - NA_WIN_ROWS)
        start = pl.multiple_of(kr0 * GRID_W, 128)
    maxima, outs = [], []
    for h in range(N_HEADS):
        qh = jnp.where((row >= h * NA_DIM) & (row < (h + 1) * NA_DIM), qt, zero)
        s_c = _dot(k_ref[ctx_off:ctx_off + n_ctx, :], qh)
        s_ref[h, n_win:n_win + n_ctx, :] = s_c
        cmax = _fold_rows(jnp.maximum, s_c, 8)
        if use_window:
            s_w = _dot(k_ref[pl.ds(start, n_win), :], qh) + bias_ref[h]
            s_ref[h, 0:n_win, :] = s_w
            cmax = jnp.maximum(cmax, _fold_rows(jnp.maximum, s_w, 8))
        maxima.append(jnp.max(cmax, axis=0, keepdims=True))
    for h in range(N_HEADS):
        for r in range(0, n_win + n_ctx, ROW_BLOCK):
            p_ref[h, r:r + ROW_BLOCK, :] = jnp.exp2((s_ref[h, r:r + ROW_BLOCK, :] - maxima[h]).astype(BF16))
    for h in range(N_HEADS):
        vrows = slice(h * NA_VROWS, (h + 1) * NA_VROWS)
        o = _dot(vt_ref[vrows, ctx_off:ctx_off + n_ctx], p_ref[h, n_win:n_win + n_ctx, :])
        if use_window:
            o = o + _dot(vt_ref[vrows, pl.ds(start, n_win)], p_ref[h, 0:n_win, :])
        outs.append(o[0:NA_DIM] / o[NA_DIM:NA_DIM + 1])
    o_ref[...] = jnp.concatenate(outs, axis=0).T.astype(o_ref.dtype)


def _na_bias_index(rows):
    nb = rows // NA_Q_ROWS
    assert rows % NA_Q_ROWS == 0 and nb >= 3 and rows >= NA_WIN_ROWS and (rows - NA_WIN_ROWS) % 2 == 0

    def block(i):
        r = i * NA_Q_ROWS + np.arange(NA_Q_ROWS)
        kr = int(np.clip(i * NA_Q_ROWS - NA_KH // 2, 0, rows - NA_WIN_ROWS)) + np.arange(NA_WIN_ROWS)
        rs = np.clip(r - NA_KH // 2, 0, rows - NA_KH)
        valid = (kr[None, :] >= rs[:, None]) & (kr[None, :] < rs[:, None] + NA_KH)
        dy = np.clip(kr[None, :] - r[:, None] + NA_KH - 1, 0, 2 * NA_KH - 2)
        return valid, dy

    blocks = [block(i) for i in range(nb)]
    for i in range(2, nb - 1):
        assert all(np.array_equal(a, b_) for a, b_ in zip(blocks[1], blocks[i]))
    cls = [blocks[0], blocks[1], blocks[nb - 1]]
    rvalid = np.stack([c[0] for c in cls]).reshape(-1)
    dy = np.stack([c[1] for c in cls]).reshape(-1)
    rsel = (dy[:, None] == np.arange(2 * NA_KH - 1)[None, :]).astype(np.float32)

    c = np.arange(GRID_W)
    cs = np.clip(c - NA_KW // 2, 0, GRID_W - NA_KW)
    cvalid = ((c[None, :] >= cs[:, None]) & (c[None, :] < cs[:, None] + NA_KW)).reshape(-1)
    dx = np.clip(c[None, :] - c[:, None] + NA_KW - 1, 0, 2 * NA_KW - 2).reshape(-1)
    csel = (np.arange(2 * NA_KW - 1)[:, None] == dx[None, :]).astype(np.float32)
    return rsel, rvalid.astype(np.float32)[:, None], csel, cvalid.astype(np.float32)[None, :]


def _na_bias_kernel(rpb_ref, rsel_ref, rmask_ref, csel_ref, cmask_ref, o_ref):
    cols = _dot_exact(rpb_ref[...], csel_ref[...])
    vals = _dot_exact(rsel_ref[...], cols)
    valid = (rmask_ref[...] * cmask_ref[...]) > 0.5
    o_ref[...] = jnp.where(valid, vals * LOG2E, NEG_BIG)


def _na_bias(rpb, rows):
    rsel, rmask, csel, cmask = (jnp.asarray(a) for a in _na_bias_index(rows))
    nh = rpb.shape[0]
    n_r, n_c = rsel.shape[0], csel.shape[1]
    full = lambda a: pl.BlockSpec(a.shape, lambda h: (0, 0))
    out = pl.pallas_call(
        _na_bias_kernel,
        grid=(nh,),
        in_specs=[pl.BlockSpec((None,) + rpb.shape[1:], lambda h: (h, 0, 0)),
                  full(rsel), full(rmask), full(csel), full(cmask)],
        out_specs=pl.BlockSpec((None, n_r, n_c), lambda h: (h, 0, 0)),
        out_shape=jax.ShapeDtypeStruct((nh, n_r, n_c), F32),
        compiler_params=_cparams(("parallel",)),
        name="na_bias",
    )(rpb.astype(F32), rsel, rmask, csel, cmask)
    out = out.reshape(nh, 3, NA_Q_ROWS, NA_WIN_ROWS, GRID_W, GRID_W)
    out = jnp.transpose(out, (1, 0, 3, 5, 2, 4))
    return out.reshape(3, nh, NA_WIN_ROWS * GRID_W, NA_Q_ROWS * GRID_W)


def _neighbourhood_attention(nqt, nk, nvt, bias, seq, out_rows):
    b, t, w = nk.shape
    tq = NA_Q_ROWS * GRID_W
    nb = seq // tq
    rows = seq // GRID_W
    n_keys = NA_WIN_ROWS * GRID_W + t - seq
    return pl.pallas_call(
        functools.partial(_na_kernel, rows=rows, ctx_off=seq, n_ctx=t - seq, use_window=True),
        grid=(b, nb),
        in_specs=[pl.BlockSpec((None, w, tq), lambda i, j: (i, 0, j)),
                  pl.BlockSpec((None, t, w), lambda i, j: (i, 0, 0)),
                  pl.BlockSpec((None, nvt.shape[1], t), lambda i, j: (i, 0, 0)),
                  pl.BlockSpec((None,) + bias.shape[1:],
                               lambda i, j: (jnp.where(j == 0, 0, jnp.where(j == nb - 1, 2, 1)), 0, 0, 0))],
        out_specs=pl.BlockSpec((None, tq, w), lambda i, j: (i, j, 0)),
        out_shape=jax.ShapeDtypeStruct((b, out_rows, w), BF16),
        scratch_shapes=[pltpu.VMEM((N_HEADS, n_keys, tq), F32), pltpu.VMEM((N_HEADS, n_keys, tq), BF16)],
        compiler_params=_cparams(("parallel", "parallel")),
        name="attn_na",
    )(nqt, nk, nvt, bias)


def _neighbourhood_attention_ctx(nqt, nk, nvt, out, seq):
    b, t, w = nk.shape
    n_ctx = t - seq
    blk = seq // n_ctx
    kern = lambda q, k, v, _, o, *scratch, **kw: _na_kernel(q, k, v, o, *scratch, **kw)
    return pl.pallas_call(
        functools.partial(kern, rows=0, ctx_off=0, n_ctx=n_ctx, use_window=False),
        grid=(b,),
        in_specs=[pl.BlockSpec((None, w, n_ctx), lambda i: (i, 0, blk)),
                  pl.BlockSpec((None, n_ctx, w), lambda i: (i, blk, 0)),
                  pl.BlockSpec((None, nvt.shape[1], n_ctx), lambda i: (i, 0, blk)),
                  pl.BlockSpec(memory_space=pl.ANY)],
        out_specs=pl.BlockSpec((None, n_ctx, w), lambda i: (i, blk, 0)),
        out_shape=jax.ShapeDtypeStruct(out.shape, out.dtype),
        scratch_shapes=[pltpu.VMEM((N_HEADS, n_ctx, n_ctx), F32), pltpu.VMEM((N_HEADS, n_ctx, n_ctx), BF16)],
        input_output_aliases={3: 0},
        compiler_params=_cparams(("parallel",)),
        name="attn_na_ctx",
    )(nqt, nk, nvt, out)


def _layer_norm(z, g, b):
    mu = jnp.mean(z, axis=-1, keepdims=True)
    zc = z - mu
    var = jnp.mean(jnp.square(zc), axis=-1, keepdims=True)
    return zc * lax.rsqrt(var + LN_EPS) * g + b


def _ffn_kernel(x_ref, ya_ref, yb_ref, yc_ref, mod_ref, wo_ref, w1_ref, w2_ref, ln_ref, o_ref, *, alpha):
    x = x_ref[...]
    a = (_dot(ya_ref[...], wo_ref[0:512, :]) + _dot(yb_ref[...], wo_ref[512:768, :])
         + _dot(yc_ref[...], wo_ref[768:1024, :]))
    x1 = _layer_norm(alpha * x + mod_ref[2:3, :] * a, ln_ref[0:1, :], ln_ref[1:2, :])
    h2 = (x1 * (1.0 + mod_ref[4:5, :]) + mod_ref[3:4, :]).astype(BF16)
    u = jnp.maximum(_dot(h2, w1_ref[...]), 0.0)
    f = _dot(jnp.square(u).astype(BF16), w2_ref[...])
    o_ref[...] = _layer_norm(alpha * x1 + mod_ref[5:6, :] * f, ln_ref[2:3, :], ln_ref[3:4, :])


def _out_ffn(xall, ya, yb, yc, mod_l, wo, w1, w2, ln, n_tok, n_ctx_row, alpha):
    b, t, d = xall.shape
    tm = TOK_TILE
    nt_all = t // tm
    nt = n_tok // tm

    def tok(width):
        return pl.BlockSpec((None, tm, width), lambda i, j: (i, j, 0))

    def const(arr):
        return pl.BlockSpec(arr.shape, lambda i, j: (0,) * arr.ndim, pipeline_mode=pl.Buffered(1))

    return pl.pallas_call(
        functools.partial(_ffn_kernel, alpha=alpha),
        grid=(b, nt),
        in_specs=[tok(d), tok(512), tok(256), tok(256),
                  pl.BlockSpec((None, 6, d), lambda i, j: (jnp.where(j == nt_all - 1, n_ctx_row, i), 0, 0)),
                  const(wo), const(w1), const(w2), const(ln)],
        out_specs=tok(d),
        out_shape=jax.ShapeDtypeStruct((b, n_tok, d), F32),
        compiler_params=_cparams(("parallel", "parallel")),
        name="out_ffn",
    )(xall, ya, yb, yc, mod_l, wo, w1, w2, ln)


def _rope_tables(seq, n_ctx):
    t = jnp.arange(seq, dtype=jnp.int32)
    row = (t // GRID_W).astype(F32)
    col = (t % GRID_W).astype(F32)

    def axial(n):
        half = n // 2
        inv = ROPE_BASE ** (-2.0 * jnp.arange(half, dtype=F32) / n)
        ar, ac = row[:, None] * inv[None, :], col[:, None] * inv[None, :]
        cos = jnp.concatenate([jnp.cos(ar), jnp.cos(ar), jnp.cos(ac), jnp.cos(ac)], axis=-1)
        sin = jnp.concatenate([-jnp.sin(ar), jnp.sin(ar), -jnp.sin(ac), jnp.sin(ac)], axis=-1)
        cos = jnp.concatenate([cos, jnp.ones((n_ctx, 2 * n), F32)], axis=0)
        sin = jnp.concatenate([sin, jnp.zeros((n_ctx, 2 * n), F32)], axis=0)
        return cos, sin

    cos_a, sin_a = axial(DA_DIM // 2)
    cos_m, sin_m = axial(MLA_ROPE // 2)
    sa = DA_DIM ** -0.5 * LOG2E
    sm = (MLA_NOPE + MLA_ROPE) ** -0.5 * LOG2E
    tt = seq + n_ctx
    ones = jnp.ones((tt, MLA_NOPE), F32)
    zpad = jnp.zeros((tt, MLA_ZPAD), F32)
    return dict(
        caq=cos_a.T * sa, saq=sin_a.T * sa,
        cak=jnp.tile(cos_a, (1, 2)), sak=jnp.tile(sin_a, (1, 2)),
        cmq=cos_m.T * sm, smq=sin_m.T * sm,
        cmk=jnp.concatenate([ones, cos_m, zpad], axis=-1),
        smk=jnp.concatenate([0.0 * ones, sin_m, zpad], axis=-1))


def _layer_weights(w_in, gq, w_uq, gkv, w_ukv):
    d = w_in.shape[0]
    o = np.cumsum([0, 512, 512, 512, MLA_Q_RANK, MLA_KV_RANK, MLA_ROPE, 256, 256, 256])
    col = lambda k: w_in[:, int(o[k]):int(o[k + 1])]
    aq, ak, av, cq, ckv, kr, nq, nk, nv = (col(k) for k in range(9))
    kr_placed = jnp.concatenate([jnp.zeros((d, MLA_NOPE), w_in.dtype), kr,
                                 jnp.zeros((d, MLA_ZPAD), w_in.dtype)], axis=1)
    wtok = jnp.concatenate([ak] + [kr_placed] * N_HEADS + [ckv, nk], axis=1).astype(BF16)
    wt = jnp.concatenate([aq, av, cq, ckv, nq, nv], axis=1).T.astype(BF16)
    uq = w_uq.reshape(MLA_Q_RANK, N_HEADS, MLA_NOPE + MLA_ROPE)
    uq = jnp.pad(uq, ((0, 0), (0, 0), (0, MLA_ZPAD)))
    ukv = w_ukv.reshape(MLA_KV_RANK, N_HEADS, MLA_NOPE + MLA_VDIM)
    uk = jnp.pad(ukv[:, :, :MLA_NOPE], ((0, 0), (0, 0), (0, HEAD_PAD - MLA_NOPE)))
    return dict(
        wtok=wtok, wt=wt,
        gqt=gq.reshape(-1, 1).astype(F32),
        wuqt=uq.reshape(MLA_Q_RANK, N_HEADS * HEAD_PAD).T.astype(BF16),
        gkv=gkv.reshape(1, -1).astype(F32), gkvt=gkv.reshape(-1, 1).astype(F32),
        wuk=uk.reshape(MLA_KV_RANK, N_HEADS * HEAD_PAD).astype(BF16),
        wuvt=ukv[:, :, MLA_NOPE:].reshape(MLA_KV_RANK, N_HEADS * MLA_VDIM).T.astype(BF16))


def _pick_key_chunk(n_k, tk_max):
    for n_c in range(2, n_k // FINE + 1, 2):
        if n_k % (n_c * FINE) == 0 and n_k // n_c <= tk_max:
            return n_k // n_c
    raise ValueError(f"no even chunking of {n_k} keys")


def kernel(x, c, ctx, c_ctx, w_mod, b_mod, w_in, da_lam_q1, da_lam_k1, da_lam_q2, da_lam_k2, da_subln_g,
           mla_q_norm_g, mla_w_uq, mla_kv_norm_g, mla_w_ukv, na_rpb, w_out, ln1_g, ln1_b, w_ff1, w_ff2,
           ln2_g, ln2_b):
    b, seq, d = x.shape
    n_ctx = ctx.shape[1]
    depth = w_mod.shape[0]
    assert seq % TOK_TILE == 0 and n_ctx == TOK_TILE and b < 8 and seq % (NA_Q_ROWS * GRID_W) == 0
    alpha = (2.0 * depth) ** 0.25
    tq = ATTN_TQ
    tk_da, tk_mla = (_pick_key_chunk(seq + n_ctx, m) for m in (DA_TK_MAX, MLA_TK_MAX))
    assert seq % tq == 0 and n_ctx % FINE == 0

    cs = jnp.zeros((8, d), F32).at[:b].set(c).at[b].set(c_ctx)
    mod = _modulation(cs, w_mod, b_mod).reshape(depth, 8, 6, d)
    tabs = _rope_tables(seq, n_ctx)
    xall = jnp.concatenate([x, ctx], axis=1)
    ctx_blk = seq // n_ctx

    for l in range(depth):
        last = l == depth - 1
        lam_init = 0.8 - 0.6 * math.exp(-0.3 * l)
        pw = _layer_weights(w_in[l], mla_q_norm_g[l], mla_w_uq[l], mla_kv_norm_g[l], mla_w_ukv[l])
        qat, ka, vat, qmt, km, vmt, nqt, nk, nvt = _projection(xall, mod[l], pw, tabs, b)

        lam_vec = jnp.stack([da_lam_q1[l], da_lam_k1[l], da_lam_q2[l], da_lam_k2[l]]).astype(F32)
        g_sub = da_subln_g[l].reshape(-1, 1).astype(F32)
        n_out = seq if last else seq + n_ctx
        ya = _dense_attention("da", qat, ka, vat, n_q=seq, tq=tq, tk=tk_da, q_blk0=0, k_blk0=0, n_k=seq + n_ctx,
                              out_rows=n_out, extra=(lam_vec, g_sub), lam_init=lam_init)
        yb = _dense_attention("mla", qmt, km, vmt, n_q=seq, tq=tq, tk=tk_mla, q_blk0=0, k_blk0=0, n_k=seq + n_ctx,
                              out_rows=n_out)
        yc = _neighbourhood_attention(nqt, nk, nvt, _na_bias(na_rpb[l], seq // GRID_W), seq, n_out)
        if not last:
            ya = _dense_attention("da", qat, ka, vat, n_q=n_ctx, tq=n_ctx, tk=n_ctx, q_blk0=ctx_blk,
                                  k_blk0=ctx_blk, n_k=n_ctx, out=ya, extra=(lam_vec, g_sub), lam_init=lam_init)
            yb = _dense_attention("mla", qmt, km, vmt, n_q=n_ctx, tq=n_ctx, tk=n_ctx, q_blk0=ctx_blk,
                                  k_blk0=ctx_blk, n_k=n_ctx, out=yb)
            yc = _neighbourhood_attention_ctx(nqt, nk, nvt, yc, seq)

        ln = jnp.stack([ln1_g[l], ln1_b[l], ln2_g[l], ln2_b[l]]).astype(F32)
        xall = _out_ffn(xall, ya, yb, yc, mod[l], w_out[l].astype(BF16), w_ff1[l].astype(BF16),
                        w_ff2[l].astype(BF16), ln, n_out, b, alpha)
    return xall
```

```python
import functools
import math

import numpy as np
import jax
import jax.numpy as jnp
from jax import lax
from jax.experimental import pallas as pl
from jax.experimental.pallas import tpu as pltpu

F32 = jnp.float32
BF16 = jnp.bfloat16

GRID_W = 64
ROPE_BASE = 10000.0
LN_EPS = 1e-6
RMS_EPS = 1e-6
SUBLN_EPS = 1e-5
N_HEADS = 4
DA_DIM = 64
DA_VDIM = 128
MLA_Q_RANK = 256
MLA_KV_RANK = 128
MLA_NOPE = 64
MLA_ROPE = 32
MLA_VDIM = 64
NA_DIM = 64
NA_KH = 8
NA_KW = 16
HEAD_PAD = 128
MLA_ZPAD = HEAD_PAD - MLA_NOPE - MLA_ROPE
ONES_ROWS = 16
DA_VROWS = DA_VDIM + ONES_ROWS
MLA_VROWS = MLA_VDIM + ONES_ROWS
NA_VROWS = NA_DIM + ONES_ROWS

LOG2E = math.log2(math.e)
NEG_BIG = -1e30

TOK_TILE = 256
ATTN_TQ = 256
DA_TK_MAX = 4224
MLA_TK_MAX = 4224
FINE = 128
ROW_BLOCK = 64
NA_Q_ROWS = 4
NA_WIN_ROWS = NA_Q_ROWS + NA_KH
VMEM_LIMIT = 56 * 1024 * 1024


def _cparams(sem, flags=None):
    return pltpu.CompilerParams(dimension_semantics=sem, vmem_limit_bytes=VMEM_LIMIT, flags=flags)


def _dot(a, b):
    return jnp.dot(a, b, preferred_element_type=F32)


def _dot_exact(a, b):
    return jnp.dot(a, b, preferred_element_type=F32, precision=lax.Precision.HIGHEST)


def _mod_kernel(c_ref, w_ref, b_ref, o_ref):
    c = c_ref[...]
    s = c / (1.0 + jnp.exp(-c))
    o_ref[...] = _dot_exact(s, w_ref[...]) + b_ref[...]


def _modulation(cs, w_mod, b_mod):
    depth, d, n = w_mod.shape
    nb = 1024
    return pl.pallas_call(
        _mod_kernel,
        grid=(depth, n // nb),
        in_specs=[pl.BlockSpec((8, d), lambda l, j: (0, 0)),
                  pl.BlockSpec((None, d, nb), lambda l, j: (l, 0, j)),
                  pl.BlockSpec((None, 1, nb), lambda l, j: (l, 0, j))],
        out_specs=pl.BlockSpec((None, 8, nb), lambda l, j: (l, 0, j)),
        out_shape=jax.ShapeDtypeStruct((depth, 8, n), F32),
        compiler_params=_cparams(("arbitrary", "arbitrary")),
        name="modulation",
    )(cs, w_mod, b_mod.reshape(depth, 1, n))


def _rms(x, axis, eps):
    return x * lax.rsqrt(jnp.mean(jnp.square(x), axis=axis, keepdims=True) + eps)


def _rope_lanes(x, lane, half, cos, sin):
    partner = jnp.where((lane & half) == 0, pltpu.roll(x, HEAD_PAD - half, 1), pltpu.roll(x, half, 1))
    return x * cos + partner * sin


def _rope_rows(x, half, cos, sin):
    partner = jnp.concatenate([x[half:2 * half], x[0:half], x[3 * half:4 * half], x[2 * half:3 * half]], axis=0)
    return x * cos + partner * sin


def _proj_kernel(x_ref, ctx_ref, mod_ref, wtok_ref, wt_ref,
                 caq_ref, saq_ref, cak_ref, sak_ref, cmq_ref, smq_ref, cmk_ref, smk_ref,
                 gqt_ref, wuqt_ref, gkv_ref, wuk_ref, gkvt_ref, wuvt_ref,
                 qat_ref, ka_ref, vat_ref, qmt_ref, km_ref, vmt_ref, nqt_ref, nk_ref, nvt_ref):
    x = jnp.where(pl.program_id(0) == pl.num_programs(0) - 1, ctx_ref[...], x_ref[...])
    h = (x * (1.0 + mod_ref[1:2, :]) + mod_ref[0:1, :]).astype(BF16)
    p = _dot(h, wtok_ref[...])
    pt = lax.dot_general(wt_ref[...], h, (((1,), (1,)), ((), ())),
                         preferred_element_type=F32)
    tm = x.shape[0]
    lane = lax.broadcasted_iota(jnp.int32, (tm, HEAD_PAD), 1)

    caq, saq = caq_ref[...], saq_ref[...]
    for g in range(2 * N_HEADS):
        rows = slice(g * DA_DIM, (g + 1) * DA_DIM)
        qat_ref[rows, :] = _rope_rows(pt[rows, :], DA_DIM // 4, caq, saq).astype(BF16)
    cak, sak = cak_ref[...], sak_ref[...]
    for g in range(N_HEADS):
        cols = slice(g * HEAD_PAD, (g + 1) * HEAD_PAD)
        ka_ref[:, cols] = _rope_lanes(p[:, cols], lane, DA_DIM // 4, cak, sak).astype(BF16)
    ones = jnp.ones((ONES_ROWS, tm), BF16)
    for g in range(N_HEADS):
        r0 = g * DA_VROWS
        vat_ref[r0:r0 + DA_VDIM, :] = pt[512 + g * DA_VDIM:512 + (g + 1) * DA_VDIM, :].astype(BF16)
        vat_ref[r0 + DA_VDIM:r0 + DA_VROWS, :] = ones

    cqnt = (_rms(pt[1024:1280, :], 0, RMS_EPS) * gqt_ref[...]).astype(BF16)
    qmt = _dot(wuqt_ref[...], cqnt)
    cmq, smq = cmq_ref[...], smq_ref[...]
    scale_m = (MLA_NOPE + MLA_ROPE) ** -0.5 * LOG2E
    for g in range(N_HEADS):
        r0 = g * HEAD_PAD
        qmt_ref[r0:r0 + MLA_NOPE, :] = (qmt[r0:r0 + MLA_NOPE, :] * scale_m).astype(BF16)
        qmt_ref[r0 + MLA_NOPE:r0 + MLA_NOPE + MLA_ROPE, :] = _rope_rows(
            qmt[r0 + MLA_NOPE:r0 + MLA_NOPE + MLA_ROPE, :], MLA_ROPE // 4, cmq, smq).astype(BF16)
        qmt_ref[r0 + MLA_NOPE + MLA_ROPE:r0 + HEAD_PAD, :] = jnp.zeros((MLA_ZPAD, tm), BF16)
    ckvn = (_rms(p[:, 1024:1152], -1, RMS_EPS) * gkv_ref[...]).astype(BF16)
    km = _dot(ckvn, wuk_ref[...]) + p[:, 512:1024]
    cmk, smk = cmk_ref[...], smk_ref[...]
    for g in range(N_HEADS):
        cols = slice(g * HEAD_PAD, (g + 1) * HEAD_PAD)
        km_ref[:, cols] = _rope_lanes(km[:, cols], lane, MLA_ROPE // 4, cmk, smk).astype(BF16)
    ckvnt = (_rms(pt[1280:1408, :], 0, RMS_EPS) * gkvt_ref[...]).astype(BF16)
    vmt = _dot(wuvt_ref[...], ckvnt)
    for g in range(N_HEADS):
        r0 = g * MLA_VROWS
        vmt_ref[r0:r0 + MLA_VDIM, :] = vmt[g * MLA_VDIM:(g + 1) * MLA_VDIM, :].astype(BF16)
        vmt_ref[r0 + MLA_VDIM:r0 + MLA_VROWS, :] = ones

    nqt_ref[...] = (pt[1408:1664, :] * (NA_DIM ** -0.5 * LOG2E)).astype(BF16)
    nk_ref[...] = p[:, 1152:1408].astype(BF16)
    for g in range(N_HEADS):
        r0 = g * NA_VROWS
        nvt_ref[r0:r0 + NA_DIM, :] = pt[1664 + g * NA_DIM:1664 + (g + 1) * NA_DIM, :].astype(BF16)
        nvt_ref[r0 + NA_DIM:r0 + NA_VROWS, :] = ones


def _projection(x_src, ctx_src, ctx_blk, seq, mod_l, pw, tabs, n_ctx_row):
    b, _, d = x_src.shape
    tm = TOK_TILE
    nt = seq // tm + 1
    t = nt * tm

    def tok(width):
        return pl.BlockSpec((None, tm, width), lambda j, i: (i, j, 0))

    def trn(height):
        return pl.BlockSpec((None, height, tm), lambda j, i: (i, 0, j))

    def const(arr):
        return pl.BlockSpec(arr.shape, lambda j, i: (0,) * arr.ndim)

    def tab_tok():
        return pl.BlockSpec((tm, HEAD_PAD), lambda j, i: (j, 0))

    def tab_trn(height):
        return pl.BlockSpec((height, tm), lambda j, i: (0, j))

    consts = [pw["gqt"], pw["wuqt"], pw["gkv"], pw["wuk"], pw["gkvt"], pw["wuvt"]]
    in_specs = [pl.BlockSpec((None, tm, d), lambda j, i: (i, jnp.minimum(j, nt - 2), 0)),
                pl.BlockSpec((None, tm, d), lambda j, i: (i, ctx_blk, 0)),
                pl.BlockSpec((None, 6, d), lambda j, i: (jnp.where(j == nt - 1, n_ctx_row, i), 0, 0)),
                const(pw["wtok"]), const(pw["wt"]),
                tab_trn(64), tab_trn(64), tab_tok(), tab_tok(),
                tab_trn(32), tab_trn(32), tab_tok(), tab_tok(),
                ] + [const(a) for a in consts]
    va_rows, vm_rows, vn_rows = N_HEADS * DA_VROWS, N_HEADS * MLA_VROWS, N_HEADS * NA_VROWS
    out_specs = [trn(512), tok(512), trn(va_rows), trn(512), tok(512), trn(vm_rows),
                 trn(256), tok(256), trn(vn_rows)]
    out_shape = [jax.ShapeDtypeStruct((b, 512, t), BF16), jax.ShapeDtypeStruct((b, t, 512), BF16),
                 jax.ShapeDtypeStruct((b, va_rows, t), BF16), jax.ShapeDtypeStruct((b, 512, t), BF16),
                 jax.ShapeDtypeStruct((b, t, 512), BF16), jax.ShapeDtypeStruct((b, vm_rows, t), BF16),
                 jax.ShapeDtypeStruct((b, 256, t), BF16), jax.ShapeDtypeStruct((b, t, 256), BF16),
                 jax.ShapeDtypeStruct((b, vn_rows, t), BF16)]
    return pl.pallas_call(
        _proj_kernel,
        grid=(nt, b),
        in_specs=in_specs,
        out_specs=out_specs,
        out_shape=out_shape,
        compiler_params=_cparams(("parallel", "parallel")),
        name="projection",
    )(x_src, ctx_src, mod_l, pw["wtok"], pw["wt"],
      tabs["caq"], tabs["saq"], tabs["cak"], tabs["sak"], tabs["cmq"], tabs["smq"], tabs["cmk"], tabs["smk"],
      *consts)


def _fold_rows(op, x, rows):
    n = x.shape[0]
    if n > ROW_BLOCK:
        acc = x[0:ROW_BLOCK]
        for g in range(1, n // ROW_BLOCK):
            acc = op(acc, x[g * ROW_BLOCK:(g + 1) * ROW_BLOCK])
        x, n = acc, ROW_BLOCK
    while n > rows:
        n //= 2
        x = op(x[:n], x[n:])
    return x


def _attn_kernel(*refs, mode, tq, n_k, tk, lam_init, aliased_out):
    o_ref, m_ref, acc_ref, s_ref, cm_ref, p_ref, al_ref = refs[-7:]
    refs = refs[:-8] if aliased_out else refs[:-7]
    if mode == "da":
        qt_ref, k_ref, vt_ref, lam_ref, g_ref = refs
        kcols = ((0, 128), (0, 128))
        vdim, vrows = DA_VDIM, ((0, DA_VROWS), (0, DA_VROWS))
    else:
        qt_ref, k_ref, vt_ref = refs
        kcols = ((0, 128), (128, 256))
        vdim, vrows = MLA_VDIM, ((0, MLA_VROWS), (MLA_VROWS, 2 * MLA_VROWS))
    n_t, n_c = qt_ref.shape[1] // tq, n_k // tk
    total = n_t * n_c
    assert n_k % tk == 0 and tk % FINE == 0 and (total == 1 or n_c % 2 == 0)

    m_ref[...] = jnp.full(m_ref.shape, NEG_BIG, F32)
    acc_ref[...] = jnp.zeros(acc_ref.shape, F32)

    def query_maps(tile):
        col = tile * tq if isinstance(tile, int) else pl.multiple_of(tile * tq, tq)
        if mode == "da":
            qt = qt_ref[:, pl.ds(col, tq)]
            row = lax.broadcasted_iota(jnp.int32, qt.shape, 0)
            zero = jnp.zeros_like(qt)
            return (jnp.where(row < DA_DIM, qt, zero), jnp.where(row >= DA_DIM, qt, zero))
        return (qt_ref[0:128, pl.ds(col, tq)], qt_ref[128:256, pl.ds(col, tq)])

    def split(f):
        if isinstance(f, int):
            return f // n_c, (f % n_c) * tk
        tile = lax.div(f, jnp.int32(n_c))
        return tile, pl.multiple_of((f - tile * n_c) * tk, 128)

    def stage(f, par):
        static = isinstance(f, int)
        do_a = not static or f < total
        do_b = not static or 1 <= f <= total
        do_c = not static or 2 <= f <= total + 1
        if do_a:
            a_tile, a_start = split(f)
            qts = query_maps(a_tile)
        if do_b:
            _, b_start = split(f - 1)
        if do_c:
            _, c_start = split(f - 2)
        for i in range(2):
            if do_b:
                m_prev = jnp.where(b_start == 0, NEG_BIG, m_ref[i])
                m_new = jnp.maximum(m_prev, cm_ref[1 - par, i])
            cmax, pv = None, None
            for r in range(0, tk, FINE):
                if do_a:
                    st = _dot(k_ref[pl.ds(a_start + r, FINE), kcols[i][0]:kcols[i][1]], qts[i])
                    s_ref[par, i, r:r + FINE, :] = st
                    mx = _fold_rows(jnp.maximum, st, 8)
                    cmax = mx if cmax is None else jnp.maximum(cmax, mx)
                if do_b:
                    for rr in range(r, r + FINE, ROW_BLOCK):
                        x = s_ref[1 - par, i, rr:rr + ROW_BLOCK, :] - m_new
                        p_ref[1 - par, i, rr:rr + ROW_BLOCK, :] = jnp.exp2(x.astype(BF16))
                if do_c and r % 256 == 0:
                    w = min(256, tk - r)
                    d = _dot(vt_ref[vrows[i][0]:vrows[i][1], pl.ds(c_start + r, w)], p_ref[par, i, r:r + w, :])
                    pv = d if pv is None else pv + d
            if do_b:
                al_ref[1 - par, i] = jnp.exp2(m_prev - m_new)
                m_ref[i] = m_new
            if do_c:
                acc_ref[i] = al_ref[par, i] * acc_ref[i] + pv
            if do_a:
                cm_ref[par, i] = jnp.max(cmax, axis=0, keepdims=True)

    def finish_tile(tile):
        o0 = acc_ref[0, 0:vdim, :] / acc_ref[0, vdim:vdim + 1, :]
        o1 = acc_ref[1, 0:vdim, :] / acc_ref[1, vdim:vdim + 1, :]
        if mode == "da":
            lv = lam_ref[...]
            lam = (jnp.exp(jnp.sum(lv[0:1] * lv[1:2], axis=-1, keepdims=True))
                   - jnp.exp(jnp.sum(lv[2:3] * lv[3:4], axis=-1, keepdims=True)) + lam_init)
            o = o0 - lam * o1
            o = _rms(o, 0, SUBLN_EPS) * g_ref[...] * (1.0 - lam_init)
        else:
            o = jnp.concatenate([o0, o1], axis=0)
        row = tile * tq if isinstance(tile, int) else pl.multiple_of(tile * tq, tq)
        o_ref[pl.ds(row, tq), :] = o.T.astype(o_ref.dtype)

    def static_stage(f):
        stage(f, f % 2)
        if f >= 2 and (f - 2) % n_c == n_c - 1:
            finish_tile((f - 2) // n_c)

    for f in range(0, min(2, total + 2)):
        static_stage(f)

    def body(j, carry):
        f0 = 2 * j
        stage(f0, 0)
        stage(f0 + 1, 1)

        if n_c == 2:
            finish_tile(j - 1)
        else:
            @pl.when(lax.rem(f0 - 1, jnp.int32(n_c)) == n_c - 1)
            def _():
                finish_tile(lax.div(f0 - 1, jnp.int32(n_c)))
        return carry

    if total > 2:
        lax.fori_loop(1, total // 2, body, 0)
    for f in range(max(2, total), total + 2):
        static_stage(f)


def _dense_attention(mode, qt, k, vt, *, n_q, tq, tk, q_blk0, k_blk0, n_k, out=None, out_rows=None,
                     extra=(), lam_init=0.0):
    b, _, t = qt.shape
    groups = N_HEADS if mode == "da" else N_HEADS // 2
    qw = 128 if mode == "da" else 256
    vblk = DA_VROWS if mode == "da" else 2 * MLA_VROWS
    acc_rows = DA_VROWS if mode == "da" else MLA_VROWS
    aliased = out is not None
    out_shape = jax.ShapeDtypeStruct(out.shape if aliased else (b, out_rows, 128 * groups), BF16)
    in_specs = [pl.BlockSpec((None, qw, n_q), lambda i, g: (i, g, q_blk0)),
                pl.BlockSpec((None, n_k, qw), lambda i, g: (i, k_blk0, g)),
                pl.BlockSpec((None, vblk, n_k), lambda i, g: (i, g, k_blk0))]
    args = [qt, k, vt]
    for a in extra:
        in_specs.append(pl.BlockSpec(a.shape, lambda i, g: (0, 0)))
        args.append(a)
    alias = {}
    if aliased:
        in_specs.append(pl.BlockSpec(memory_space=pl.ANY))
        alias = {len(args): 0}
        args.append(out)
    return pl.pallas_call(
        functools.partial(_attn_kernel, mode=mode, tq=tq, n_k=n_k, tk=tk, lam_init=lam_init,
                          aliased_out=aliased),
        grid=(b, groups),
        in_specs=in_specs,
        out_specs=pl.BlockSpec((None, n_q, 128), lambda i, g: (i, q_blk0, g)),
        out_shape=out_shape,
        scratch_shapes=[pltpu.VMEM((2, 1, tq), F32),
                        pltpu.VMEM((2, acc_rows, tq), F32), pltpu.VMEM((2, 2, tk, tq), F32),
                        pltpu.VMEM((2, 2, 1, tq), F32), pltpu.VMEM((2, 2, tk, tq), BF16),
                        pltpu.VMEM((2, 2, 1, tq), F32)],
        input_output_aliases=alias,
        compiler_params=_cparams(("parallel", "parallel")),
        name="attn_" + mode,
    )(*args)


def _na_kernel(*refs, rows, ctx_off, n_ctx, use_window):
    if use_window:
        qt_ref, k_ref, vt_ref, bias_ref, o_ref, s_ref, p_ref = refs
    else:
        qt_ref, k_ref, vt_ref, o_ref, s_ref, p_ref = refs
    qt = qt_ref[...]
    row = lax.broadcasted_iota(jnp.int32, qt.shape, 0)
    zero = jnp.zeros_like(qt)
    n_win = NA_WIN_ROWS * GRID_W if use_window else 0
    if use_window:
        i = pl.program_id(1)
        kr0 = jnp.clip(i * NA_Q_ROWS - NA_KH // 2, 0, rows - NA_WIN_ROWS)
        start = pl.multiple_of(kr0 * GRID_W, 128)
    maxima, outs = [], []
    for h in range(N_HEADS):
        qh = jnp.where((row >= h * NA_DIM) & (row < (h + 1) * NA_DIM), qt, zero)
        s_c = _dot(k_ref[ctx_off:ctx_off + n_ctx, :], qh)
        s_ref[h, n_win:n_win + n_ctx, :] = s_c
        cmax = _fold_rows(jnp.maximum, s_c, 8)
        if use_window:
            s_w = _dot(k_ref[pl.ds(start, n_win), :], qh) + bias_ref[h]
            s_ref[h, 0:n_win, :] = s_w
            cmax = jnp.maximum(cmax, _fold_rows(jnp.maximum, s_w, 8))
        maxima.append(jnp.max(cmax, axis=0, keepdims=True))
    for h in range(N_HEADS):
        for r in range(0, n_win + n_ctx, ROW_BLOCK):
            p_ref[h, r:r + ROW_BLOCK, :] = jnp.exp2((s_ref[h, r:r + ROW_BLOCK, :] - maxima[h]).astype(BF16))
    for h in range(N_HEADS):
        vrows = slice(h * NA_VROWS, (h + 1) * NA_VROWS)
        o = _dot(vt_ref[vrows, ctx_off:ctx_off + n_ctx], p_ref[h, n_win:n_win + n_ctx, :])
        if use_window:
            o = o + _dot(vt_ref[vrows, pl.ds(start, n_win)], p_ref[h, 0:n_win, :])
        outs.append(o[0:NA_DIM] / o[NA_DIM:NA_DIM + 1])
    o_ref[...] = jnp.concatenate(outs, axis=0).T.astype(o_ref.dtype)


def _na_bias_index(rows):
    nb = rows // NA_Q_ROWS
    assert rows % NA_Q_ROWS == 0 and nb >= 3 and rows >= NA_WIN_ROWS and (rows - NA_WIN_ROWS) % 2 == 0

    def block(i):
        r = i * NA_Q_ROWS + np.arange(NA_Q_ROWS)
        kr = int(np.clip(i * NA_Q_ROWS - NA_KH // 2, 0, rows - NA_WIN_ROWS)) + np.arange(NA_WIN_ROWS)
        rs = np.clip(r - NA_KH // 2, 0, rows - NA_KH)
        valid = (kr[None, :] >= rs[:, None]) & (kr[None, :] < rs[:, None] + NA_KH)
        dy = np.clip(kr[None, :] - r[:, None] + NA_KH - 1, 0, 2 * NA_KH - 2)
        return valid, dy

    blocks = [block(i) for i in range(nb)]
    for i in range(2, nb - 1):
        assert all(np.array_equal(a, b_) for a, b_ in zip(blocks[1], blocks[i]))
    cls = [blocks[0], blocks[1], blocks[nb - 1]]
    rvalid = np.stack([c[0] for c in cls]).reshape(-1)
    dy = np.stack([c[1] for c in cls]).reshape(-1)
    rsel = (dy[:, None] == np.arange(2 * NA_KH - 1)[None, :]).astype(np.float32)

    c = np.arange(GRID_W)
    cs = np.clip(c - NA_KW // 2, 0, GRID_W - NA_KW)
    cvalid = ((c[None, :] >= cs[:, None]) & (c[None, :] < cs[:, None] + NA_KW)).reshape(-1)
    dx = np.clip(c[None, :] - c[:, None] + NA_KW - 1, 0, 2 * NA_KW - 2).reshape(-1)
    csel = (np.arange(2 * NA_KW - 1)[:, None] == dx[None, :]).astype(np.float32)
    return rsel, rvalid.astype(np.float32)[:, None], csel, cvalid.astype(np.float32)[None, :]


def _na_bias_kernel(rpb_ref, rsel_ref, rmask_ref, csel_ref, cmask_ref, o_ref):
    cols = _dot_exact(rpb_ref[...], csel_ref[...])
    vals = _dot_exact(rsel_ref[...], cols)
    valid = (rmask_ref[...] * cmask_ref[...]) > 0.5
    o_ref[...] = jnp.where(valid, vals * LOG2E, NEG_BIG)


def _na_bias(rpb, rows):
    rsel, rmask, csel, cmask = (jnp.asarray(a) for a in _na_bias_index(rows))
    nh = rpb.shape[0]
    n_r, n_c = rsel.shape[0], csel.shape[1]
    full = lambda a: pl.BlockSpec(a.shape, lambda h: (0, 0))
    out = pl.pallas_call(
        _na_bias_kernel,
        grid=(nh,),
        in_specs=[pl.BlockSpec((None,) + rpb.shape[1:], lambda h: (h, 0, 0)),
                  full(rsel), full(rmask), full(csel), full(cmask)],
        out_specs=pl.BlockSpec((None, n_r, n_c), lambda h: (h, 0, 0)),
        out_shape=jax.ShapeDtypeStruct((nh, n_r, n_c), F32),
        compiler_params=_cparams(("parallel",)),
        name="na_bias",
    )(rpb.astype(F32), rsel, rmask, csel, cmask)
    out = out.reshape(nh, 3, NA_Q_ROWS, NA_WIN_ROWS, GRID_W, GRID_W)
    out = jnp.transpose(out, (1, 0, 3, 5, 2, 4))
    return out.reshape(3, nh, NA_WIN_ROWS * GRID_W, NA_Q_ROWS * GRID_W)


def _neighbourhood_attention(nqt, nk, nvt, bias, seq, out_rows):
    b, t, w = nk.shape
    tq = NA_Q_ROWS * GRID_W
    nb = seq // tq
    rows = seq // GRID_W
    n_keys = NA_WIN_ROWS * GRID_W + t - seq
    return pl.pallas_call(
        functools.partial(_na_kernel, rows=rows, ctx_off=seq, n_ctx=t - seq, use_window=True),
        grid=(b, nb),
        in_specs=[pl.BlockSpec((None, w, tq), lambda i, j: (i, 0, j)),
                  pl.BlockSpec((None, t, w), lambda i, j: (i, 0, 0)),
                  pl.BlockSpec((None, nvt.shape[1], t), lambda i, j: (i, 0, 0)),
                  pl.BlockSpec((None,) + bias.shape[1:],
                               lambda i, j: (jnp.where(j == 0, 0, jnp.where(j == nb - 1, 2, 1)), 0, 0, 0))],
        out_specs=pl.BlockSpec((None, tq, w), lambda i, j: (i, j, 0)),
        out_shape=jax.ShapeDtypeStruct((b, out_rows, w), BF16),
        scratch_shapes=[pltpu.VMEM((N_HEADS, n_keys, tq), F32), pltpu.VMEM((N_HEADS, n_keys, tq), BF16)],
        compiler_params=_cparams(("parallel", "parallel")),
        name="attn_na",
    )(nqt, nk, nvt, bias)


def _neighbourhood_attention_ctx(nqt, nk, nvt, out, seq):
    b, t, w = nk.shape
    n_ctx = t - seq
    blk = seq // n_ctx
    kern = lambda q, k, v, _, o, *scratch, **kw: _na_kernel(q, k, v, o, *scratch, **kw)
    return pl.pallas_call(
        functools.partial(kern, rows=0, ctx_off=0, n_ctx=n_ctx, use_window=False),
        grid=(b,),
        in_specs=[pl.BlockSpec((None, w, n_ctx), lambda i: (i, 0, blk)),
                  pl.BlockSpec((None, n_ctx, w), lambda i: (i, blk, 0)),
                  pl.BlockSpec((None, nvt.shape[1], n_ctx), lambda i: (i, 0, blk)),
                  pl.BlockSpec(memory_space=pl.ANY)],
        out_specs=pl.BlockSpec((None, n_ctx, w), lambda i: (i, blk, 0)),
        out_shape=jax.ShapeDtypeStruct(out.shape, out.dtype),
        scratch_shapes=[pltpu.VMEM((N_HEADS, n_ctx, n_ctx), F32), pltpu.VMEM((N_HEADS, n_ctx, n_ctx), BF16)],
        input_output_aliases={3: 0},
        compiler_params=_cparams(("parallel",)),
        name="attn_na_ctx",
    )(nqt, nk, nvt, out)


def _layer_norm(z, g, b):
    mu = jnp.mean(z, axis=-1, keepdims=True)
    zc = z - mu
    var = jnp.mean(jnp.square(zc), axis=-1, keepdims=True)
    return zc * lax.rsqrt(var + LN_EPS) * g + b


def _ffn_kernel(x_ref, ctx_ref, ya_ref, yb_ref, yc_ref, mod_ref, wo_ref, w1_ref, w2_ref, ln_ref, o_ref, *,
                alpha, ctx_tile):
    x = jnp.where(pl.program_id(1) == ctx_tile, ctx_ref[...], x_ref[...])
    a = (_dot(ya_ref[...], wo_ref[0:512, :]) + _dot(yb_ref[...], wo_ref[512:768, :])
         + _dot(yc_ref[...], wo_ref[768:1024, :]))
    x1 = _layer_norm(alpha * x + mod_ref[2:3, :] * a, ln_ref[0:1, :], ln_ref[1:2, :])
    h2 = (x1 * (1.0 + mod_ref[4:5, :]) + mod_ref[3:4, :]).astype(BF16)
    u = jnp.maximum(_dot(h2, w1_ref[...]), 0.0)
    f = _dot(jnp.square(u).astype(BF16), w2_ref[...])
    o_ref[...] = _layer_norm(alpha * x1 + mod_ref[5:6, :] * f, ln_ref[2:3, :], ln_ref[3:4, :])


def _out_ffn(x_src, ctx_src, ctx_blk, seq, ya, yb, yc, mod_l, wo, w1, w2, ln, n_tok, n_ctx_row, alpha):
    b, _, d = x_src.shape
    tm = TOK_TILE
    ctx_tile = seq // tm
    nt = n_tok // tm

    def tok(width):
        return pl.BlockSpec((None, tm, width), lambda i, j: (i, j, 0))

    def const(arr):
        return pl.BlockSpec(arr.shape, lambda i, j: (0,) * arr.ndim, pipeline_mode=pl.Buffered(1))

    return pl.pallas_call(
        functools.partial(_ffn_kernel, alpha=alpha, ctx_tile=ctx_tile),
        grid=(b, nt),
        in_specs=[pl.BlockSpec((None, tm, d), lambda i, j: (i, jnp.minimum(j, ctx_tile - 1), 0)),
                  pl.BlockSpec((None, tm, d), lambda i, j: (i, ctx_blk, 0)),
                  tok(512), tok(256), tok(256),
                  pl.BlockSpec((None, 6, d), lambda i, j: (jnp.where(j == ctx_tile, n_ctx_row, i), 0, 0)),
                  const(wo), const(w1), const(w2), const(ln)],
        out_specs=tok(d),
        out_shape=jax.ShapeDtypeStruct((b, n_tok, d), F32),
        compiler_params=_cparams(("parallel", "parallel")),
        name="out_ffn",
    )(x_src, ctx_src, ya, yb, yc, mod_l, wo, w1, w2, ln)


def _rope_tables(seq, n_ctx):
    t = jnp.arange(seq, dtype=jnp.int32)
    row = (t // GRID_W).astype(F32)
    col = (t % GRID_W).astype(F32)

    def axial(n):
        half = n // 2
        inv = ROPE_BASE ** (-2.0 * jnp.arange(half, dtype=F32) / n)
        ar, ac = row[:, None] * inv[None, :], col[:, None] * inv[None, :]
        cos = jnp.concatenate([jnp.cos(ar), jnp.cos(ar), jnp.cos(ac), jnp.cos(ac)], axis=-1)
        sin = jnp.concatenate([-jnp.sin(ar), jnp.sin(ar), -jnp.sin(ac), jnp.sin(ac)], axis=-1)
        cos = jnp.concatenate([cos, jnp.ones((n_ctx, 2 * n), F32)], axis=0)
        sin = jnp.concatenate([sin, jnp.zeros((n_ctx, 2 * n), F32)], axis=0)
        return cos, sin

    cos_a, sin_a = axial(DA_DIM // 2)
    cos_m, sin_m = axial(MLA_ROPE // 2)
    sa = DA_DIM ** -0.5 * LOG2E
    sm = (MLA_NOPE + MLA_ROPE) ** -0.5 * LOG2E
    tt = seq + n_ctx
    ones = jnp.ones((tt, MLA_NOPE), F32)
    zpad = jnp.zeros((tt, MLA_ZPAD), F32)
    return dict(
        caq=cos_a.T * sa, saq=sin_a.T * sa,
        cak=jnp.tile(cos_a, (1, 2)), sak=jnp.tile(sin_a, (1, 2)),
        cmq=cos_m.T * sm, smq=sin_m.T * sm,
        cmk=jnp.concatenate([ones, cos_m, zpad], axis=-1),
        smk=jnp.concatenate([0.0 * ones, sin_m, zpad], axis=-1))


def _layer_weights(w_in, gq, w_uq, gkv, w_ukv):
    d = w_in.shape[0]
    o = np.cumsum([0, 512, 512, 512, MLA_Q_RANK, MLA_KV_RANK, MLA_ROPE, 256, 256, 256])
    col = lambda k: w_in[:, int(o[k]):int(o[k + 1])]
    aq, ak, av, cq, ckv, kr, nq, nk, nv = (col(k) for k in range(9))
    kr_placed = jnp.concatenate([jnp.zeros((d, MLA_NOPE), w_in.dtype), kr,
                                 jnp.zeros((d, MLA_ZPAD), w_in.dtype)], axis=1)
    wtok = jnp.concatenate([ak] + [kr_placed] * N_HEADS + [ckv, nk], axis=1).astype(BF16)
    wt = jnp.concatenate([aq, av, cq, ckv, nq, nv], axis=1).T.astype(BF16)
    uq = w_uq.reshape(MLA_Q_RANK, N_HEADS, MLA_NOPE + MLA_ROPE)
    uq = jnp.pad(uq, ((0, 0), (0, 0), (0, MLA_ZPAD)))
    ukv = w_ukv.reshape(MLA_KV_RANK, N_HEADS, MLA_NOPE + MLA_VDIM)
    uk = jnp.pad(ukv[:, :, :MLA_NOPE], ((0, 0), (0, 0), (0, HEAD_PAD - MLA_NOPE)))
    return dict(
        wtok=wtok, wt=wt,
        gqt=gq.reshape(-1, 1).astype(F32),
        wuqt=uq.reshape(MLA_Q_RANK, N_HEADS * HEAD_PAD).T.astype(BF16),
        gkv=gkv.reshape(1, -1).astype(F32), gkvt=gkv.reshape(-1, 1).astype(F32),
        wuk=uk.reshape(MLA_KV_RANK, N_HEADS * HEAD_PAD).astype(BF16),
        wuvt=ukv[:, :, MLA_NOPE:].reshape(MLA_KV_RANK, N_HEADS * MLA_VDIM).T.astype(BF16))


def _pick_key_chunk(n_k, tk_max):
    for n_c in range(2, n_k // FINE + 1, 2):
        if n_k % (n_c * FINE) == 0 and n_k // n_c <= tk_max:
            return n_k // n_c
    raise ValueError(f"no even chunking of {n_k} keys")


def kernel(x, c, ctx, c_ctx, w_mod, b_mod, w_in, da_lam_q1, da_lam_k1, da_lam_q2, da_lam_k2, da_subln_g,
           mla_q_norm_g, mla_w_uq, mla_kv_norm_g, mla_w_ukv, na_rpb, w_out, ln1_g, ln1_b, w_ff1, w_ff2,
           ln2_g, ln2_b):
    b, seq, d = x.shape
    n_ctx = ctx.shape[1]
    depth = w_mod.shape[0]
    assert seq % TOK_TILE == 0 and n_ctx == TOK_TILE and b < 8 and seq % (NA_Q_ROWS * GRID_W) == 0
    alpha = (2.0 * depth) ** 0.25
    tq = ATTN_TQ
    tk_da, tk_mla = (_pick_key_chunk(seq + n_ctx, m) for m in (DA_TK_MAX, MLA_TK_MAX))
    assert seq % tq == 0 and n_ctx % FINE == 0

    cs = jnp.zeros((8, d), F32).at[:b].set(c).at[b].set(c_ctx)
    mod = _modulation(cs, w_mod, b_mod).reshape(depth, 8, 6, d)
    tabs = _rope_tables(seq, n_ctx)
    ctx_blk = seq // n_ctx
    x_src, ctx_src, src_blk = x, ctx, 0

    for l in range(depth):
        last = l == depth - 1
        lam_init = 0.8 - 0.6 * math.exp(-0.3 * l)
        pw = _layer_weights(w_in[l], mla_q_norm_g[l], mla_w_uq[l], mla_kv_norm_g[l], mla_w_ukv[l])
        qat, ka, vat, qmt, km, vmt, nqt, nk, nvt = _projection(x_src, ctx_src, src_blk, seq, mod[l], pw,
                                                                tabs, b)

        lam_vec = jnp.stack([da_lam_q1[l], da_lam_k1[l], da_lam_q2[l], da_lam_k2[l]]).astype(F32)
        g_sub = da_subln_g[l].reshape(-1, 1).astype(F32)
        n_out = seq if last else seq + n_ctx
        ya = _dense_attention("da", qat, ka, vat, n_q=seq, tq=tq, tk=tk_da, q_blk0=0, k_blk0=0, n_k=seq + n_ctx,
                              out_rows=n_out, extra=(lam_vec, g_sub), lam_init=lam_init)
        yb = _dense_attention("mla", qmt, km, vmt, n_q=seq, tq=tq, tk=tk_mla, q_blk0=0, k_blk0=0, n_k=seq + n_ctx,
                              out_rows=n_out)
        yc = _neighbourhood_attention(nqt, nk, nvt, _na_bias(na_rpb[l], seq // GRID_W), seq, n_out)
        if not last:
            ya = _dense_attention("da", qat, ka, vat, n_q=n_ctx, tq=n_ctx, tk=n_ctx, q_blk0=ctx_blk,
                                  k_blk0=ctx_blk, n_k=n_ctx, out=ya, extra=(lam_vec, g_sub), lam_init=lam_init)
            yb = _dense_attention("mla", qmt, km, vmt, n_q=n_ctx, tq=n_ctx, tk=n_ctx, q_blk0=ctx_blk,
                                  k_blk0=ctx_blk, n_k=n_ctx, out=yb)
            yc = _neighbourhood_attention_ctx(nqt, nk, nvt, yc, seq)

        ln = jnp.stack([ln1_g[l], ln1_b[l], ln2_g[l], ln2_b[l]]).astype(F32)
        x_src = _out_ffn(x_src, ctx_src, src_blk, seq, ya, yb, yc, mod[l], w_out[l].astype(BF16),
                         w_ff1[l].astype(BF16), w_ff2[l].astype(BF16), ln, n_out, b, alpha)
        ctx_src, src_blk = x_src, ctx_blk
    return x_src
```

```python
import functools
import math

import numpy as np
import jax
import jax.numpy as jnp
from jax import lax
from jax.experimental import pallas as pl
from jax.experimental.pallas import tpu as pltpu

F32 = jnp.float32
BF16 = jnp.bfloat16

GRID_W = 64
ROPE_BASE = 10000.0
LN_EPS = 1e-6
RMS_EPS = 1e-6
SUBLN_EPS = 1e-5
N_HEADS = 4
DA_DIM = 64
DA_VDIM = 128
MLA_Q_RANK = 256
MLA_KV_RANK = 128
MLA_NOPE = 64
MLA_ROPE = 32
MLA_VDIM = 64
NA_DIM = 64
NA_KH = 8
NA_KW = 16
HEAD_PAD = 128
MLA_ZPAD = HEAD_PAD - MLA_NOPE - MLA_ROPE
ONES_ROWS = 16
DA_VROWS = DA_VDIM + ONES_ROWS
MLA_VROWS = MLA_VDIM + ONES_ROWS
NA_VROWS = NA_DIM + ONES_ROWS

LOG2E = math.log2(math.e)
NEG_BIG = -1e30

TOK_TILE = 256
ATTN_TQ = 256
DA_TK_MAX = 4224
MLA_TK_MAX = 4224
FINE = 128
ROW_BLOCK = 64
NA_Q_ROWS = 4
NA_WIN_ROWS = NA_Q_ROWS + NA_KH
VMEM_LIMIT = 56 * 1024 * 1024


def _cparams(sem, flags=None):
    return pltpu.CompilerParams(dimension_semantics=sem, vmem_limit_bytes=VMEM_LIMIT, flags=flags)


def _dot(a, b):
    return jnp.dot(a, b, preferred_element_type=F32)


def _dot_exact(a, b):
    return jnp.dot(a, b, preferred_element_type=F32, precision=lax.Precision.HIGHEST)


def _mod_kernel(c_ref, w_ref, b_ref, o_ref):
    c = c_ref[...]
    s = c / (1.0 + jnp.exp(-c))
    o_ref[...] = _dot_exact(s, w_ref[...]) + b_ref[...]


def _modulation(cs, w_mod, b_mod):
    depth, d, n = w_mod.shape
    nb = 1024
    return pl.pallas_call(
        _mod_kernel,
        grid=(depth, n // nb),
        in_specs=[pl.BlockSpec((8, d), lambda l, j: (0, 0)),
                  pl.BlockSpec((None, d, nb), lambda l, j: (l, 0, j)),
                  pl.BlockSpec((None, 1, nb), lambda l, j: (l, 0, j))],
        out_specs=pl.BlockSpec((None, 8, nb), lambda l, j: (l, 0, j)),
        out_shape=jax.ShapeDtypeStruct((depth, 8, n), F32),
        compiler_params=_cparams(("arbitrary", "arbitrary")),
        name="modulation",
    )(cs, w_mod, b_mod.reshape(depth, 1, n))


def _rms(x, axis, eps):
    return x * lax.rsqrt(jnp.mean(jnp.square(x), axis=axis, keepdims=True) + eps)


def _rope_lanes(x, lane, half, cos, sin):
    partner = jnp.where((lane & half) == 0, pltpu.roll(x, HEAD_PAD - half, 1), pltpu.roll(x, half, 1))
    return x * cos + partner * sin


def _rope_rows(x, half, cos, sin):
    partner = jnp.concatenate([x[half:2 * half], x[0:half], x[3 * half:4 * half], x[2 * half:3 * half]], axis=0)
    return x * cos + partner * sin


def _proj_kernel(x_ref, ctx_ref, mod_ref, wtok_ref, wt_ref,
                 caq_ref, saq_ref, cak_ref, sak_ref, cmq_ref, smq_ref, cmk_ref, smk_ref,
                 gqt_ref, wuqt_ref, gkv_ref, wuk_ref, gkvt_ref, wuvt_ref,
                 qat_ref, ka_ref, vat_ref, qmt_ref, km_ref, vmt_ref, nqt_ref, nk_ref, nvt_ref):
    x = jnp.where(pl.program_id(0) == pl.num_programs(0) - 1, ctx_ref[...], x_ref[...])
    h = (x * (1.0 + mod_ref[1:2, :]) + mod_ref[0:1, :]).astype(BF16)
    p = _dot(h, wtok_ref[...])
    pt = lax.dot_general(wt_ref[...], h, (((1,), (1,)), ((), ())),
                         preferred_element_type=F32)
    tm = x.shape[0]
    lane = lax.broadcasted_iota(jnp.int32, (tm, HEAD_PAD), 1)

    caq, saq = caq_ref[...], saq_ref[...]
    for g in range(2 * N_HEADS):
        rows = slice(g * DA_DIM, (g + 1) * DA_DIM)
        qat_ref[rows, :] = _rope_rows(pt[rows, :], DA_DIM // 4, caq, saq).astype(BF16)
    cak, sak = cak_ref[...], sak_ref[...]
    for g in range(N_HEADS):
        cols = slice(g * HEAD_PAD, (g + 1) * HEAD_PAD)
        ka_ref[:, cols] = _rope_lanes(p[:, cols], lane, DA_DIM // 4, cak, sak).astype(BF16)
    ones = jnp.ones((ONES_ROWS, tm), BF16)
    for g in range(N_HEADS):
        r0 = g * DA_VROWS
        vat_ref[r0:r0 + DA_VDIM, :] = pt[512 + g * DA_VDIM:512 + (g + 1) * DA_VDIM, :].astype(BF16)
        vat_ref[r0 + DA_VDIM:r0 + DA_VROWS, :] = ones

    cqnt = (_rms(pt[1024:1280, :], 0, RMS_EPS) * gqt_ref[...]).astype(BF16)
    qmt = _dot(wuqt_ref[...], cqnt)
    cmq, smq = cmq_ref[...], smq_ref[...]
    scale_m = (MLA_NOPE + MLA_ROPE) ** -0.5 * LOG2E
    for g in range(N_HEADS):
        r0 = g * HEAD_PAD
        qmt_ref[r0:r0 + MLA_NOPE, :] = (qmt[r0:r0 + MLA_NOPE, :] * scale_m).astype(BF16)
        qmt_ref[r0 + MLA_NOPE:r0 + MLA_NOPE + MLA_ROPE, :] = _rope_rows(
            qmt[r0 + MLA_NOPE:r0 + MLA_NOPE + MLA_ROPE, :], MLA_ROPE // 4, cmq, smq).astype(BF16)
        qmt_ref[r0 + MLA_NOPE + MLA_ROPE:r0 + HEAD_PAD, :] = jnp.zeros((MLA_ZPAD, tm), BF16)
    ckvn = (_rms(p[:, 1024:1152], -1, RMS_EPS) * gkv_ref[...]).astype(BF16)
    km = _dot(ckvn, wuk_ref[...]) + p[:, 512:1024]
    cmk, smk = cmk_ref[...], smk_ref[...]
    for g in range(N_HEADS):
        cols = slice(g * HEAD_PAD, (g + 1) * HEAD_PAD)
        km_ref[:, cols] = _rope_lanes(km[:, cols], lane, MLA_ROPE // 4, cmk, smk).astype(BF16)
    ckvnt = (_rms(pt[1280:1408, :], 0, RMS_EPS) * gkvt_ref[...]).astype(BF16)
    vmt = _dot(wuvt_ref[...], ckvnt)
    for g in range(N_HEADS):
        r0 = g * MLA_VROWS
        vmt_ref[r0:r0 + MLA_VDIM, :] = vmt[g * MLA_VDIM:(g + 1) * MLA_VDIM, :].astype(BF16)
        vmt_ref[r0 + MLA_VDIM:r0 + MLA_VROWS, :] = ones

    nqt_ref[...] = (pt[1408:1664, :] * (NA_DIM ** -0.5 * LOG2E)).astype(BF16)
    nk_ref[...] = p[:, 1152:1408].astype(BF16)
    for g in range(N_HEADS):
        r0 = g * NA_VROWS
        nvt_ref[r0:r0 + NA_DIM, :] = pt[1664 + g * NA_DIM:1664 + (g + 1) * NA_DIM, :].astype(BF16)
        nvt_ref[r0 + NA_DIM:r0 + NA_VROWS, :] = ones


def _projection(x_src, ctx_src, ctx_blk, seq, mod_l, pw, tabs, n_ctx_row):
    b, _, d = x_src.shape
    tm = TOK_TILE
    nt = seq // tm + 1
    t = nt * tm

    def tok(width):
        return pl.BlockSpec((None, tm, width), lambda j, i: (i, j, 0))

    def trn(height):
        return pl.BlockSpec((None, height, tm), lambda j, i: (i, 0, j))

    def const(arr):
        return pl.BlockSpec(arr.shape, lambda j, i: (0,) * arr.ndim)

    def tab_tok():
        return pl.BlockSpec((tm, HEAD_PAD), lambda j, i: (j, 0))

    def tab_trn(height):
        return pl.BlockSpec((height, tm), lambda j, i: (0, j))

    consts = [pw["gqt"], pw["wuqt"], pw["gkv"], pw["wuk"], pw["gkvt"], pw["wuvt"]]
    in_specs = [pl.BlockSpec((None, tm, d), lambda j, i: (i, jnp.minimum(j, nt - 2), 0)),
                pl.BlockSpec((None, tm, d), lambda j, i: (i, ctx_blk, 0)),
                pl.BlockSpec((None, 6, d), lambda j, i: (jnp.where(j == nt - 1, n_ctx_row, i), 0, 0)),
                const(pw["wtok"]), const(pw["wt"]),
                tab_trn(64), tab_trn(64), tab_tok(), tab_tok(),
                tab_trn(32), tab_trn(32), tab_tok(), tab_tok(),
                ] + [const(a) for a in consts]
    va_rows, vm_rows, vn_rows = N_HEADS * DA_VROWS, N_HEADS * MLA_VROWS, N_HEADS * NA_VROWS
    out_specs = [trn(512), tok(512), trn(va_rows), trn(512), tok(512), trn(vm_rows),
                 trn(256), tok(256), trn(vn_rows)]
    out_shape = [jax.ShapeDtypeStruct((b, 512, t), BF16), jax.ShapeDtypeStruct((b, t, 512), BF16),
                 jax.ShapeDtypeStruct((b, va_rows, t), BF16), jax.ShapeDtypeStruct((b, 512, t), BF16),
                 jax.ShapeDtypeStruct((b, t, 512), BF16), jax.ShapeDtypeStruct((b, vm_rows, t), BF16),
                 jax.ShapeDtypeStruct((b, 256, t), BF16), jax.ShapeDtypeStruct((b, t, 256), BF16),
                 jax.ShapeDtypeStruct((b, vn_rows, t), BF16)]
    return pl.pallas_call(
        _proj_kernel,
        grid=(nt, b),
        in_specs=in_specs,
        out_specs=out_specs,
        out_shape=out_shape,
        compiler_params=_cparams(("parallel", "parallel")),
        name="projection",
    )(x_src, ctx_src, mod_l, pw["wtok"], pw["wt"],
      tabs["caq"], tabs["saq"], tabs["cak"], tabs["sak"], tabs["cmq"], tabs["smq"], tabs["cmk"], tabs["smk"],
      *consts)


def _fold_rows(op, x, rows):
    n = x.shape[0]
    if n > ROW_BLOCK:
        acc = x[0:ROW_BLOCK]
        for g in range(1, n // ROW_BLOCK):
            acc = op(acc, x[g * ROW_BLOCK:(g + 1) * ROW_BLOCK])
        x, n = acc, ROW_BLOCK
    while n > rows:
        n //= 2
        x = op(x[:n], x[n:])
    return x


def _attn_kernel(*refs, mode, tq, n_k, tk, lam_init):
    o_ref, m_ref, acc_ref, s_ref, cm_ref, p_ref, al_ref = refs[-7:]
    refs = refs[:-7]
    if mode == "da":
        qt_ref, k_ref, vt_ref, lam_ref, g_ref = refs
        kcols = ((0, 128), (0, 128))
        vdim, vrows = DA_VDIM, ((0, DA_VROWS), (0, DA_VROWS))
    else:
        qt_ref, k_ref, vt_ref = refs
        kcols = ((0, 128), (128, 256))
        vdim, vrows = MLA_VDIM, ((0, MLA_VROWS), (MLA_VROWS, 2 * MLA_VROWS))
    n_t, n_c = qt_ref.shape[1] // tq, n_k // tk
    total = n_t * n_c
    assert n_k % tk == 0 and tk % FINE == 0 and (total == 1 or n_c % 2 == 0)

    m_ref[...] = jnp.full(m_ref.shape, NEG_BIG, F32)
    acc_ref[...] = jnp.zeros(acc_ref.shape, F32)

    def query_maps(tile):
        col = tile * tq if isinstance(tile, int) else pl.multiple_of(tile * tq, tq)
        if mode == "da":
            qt = qt_ref[:, pl.ds(col, tq)]
            row = lax.broadcasted_iota(jnp.int32, qt.shape, 0)
            zero = jnp.zeros_like(qt)
            return (jnp.where(row < DA_DIM, qt, zero), jnp.where(row >= DA_DIM, qt, zero))
        return (qt_ref[0:128, pl.ds(col, tq)], qt_ref[128:256, pl.ds(col, tq)])

    def split(f):
        if isinstance(f, int):
            return f // n_c, (f % n_c) * tk
        tile = lax.div(f, jnp.int32(n_c))
        return tile, pl.multiple_of((f - tile * n_c) * tk, 128)

    def stage(f, par):
        static = isinstance(f, int)
        do_a = not static or f < total
        do_b = not static or 1 <= f <= total
        do_c = not static or 2 <= f <= total + 1
        if do_a:
            a_tile, a_start = split(f)
            qts = query_maps(a_tile)
        if do_b:
            _, b_start = split(f - 1)
        if do_c:
            _, c_start = split(f - 2)
        for i in range(2):
            if do_b:
                m_prev = jnp.where(b_start == 0, NEG_BIG, m_ref[i])
                m_new = jnp.maximum(m_prev, cm_ref[1 - par, i])
            cmax, pv = None, None
            for r in range(0, tk, FINE):
                if do_a:
                    st = _dot(k_ref[pl.ds(a_start + r, FINE), kcols[i][0]:kcols[i][1]], qts[i])
                    s_ref[par, i, r:r + FINE, :] = st
                    mx = _fold_rows(jnp.maximum, st, 8)
                    cmax = mx if cmax is None else jnp.maximum(cmax, mx)
                if do_b:
                    for rr in range(r, r + FINE, ROW_BLOCK):
                        x = s_ref[1 - par, i, rr:rr + ROW_BLOCK, :] - m_new
                        p_ref[1 - par, i, rr:rr + ROW_BLOCK, :] = jnp.exp2(x.astype(BF16))
                if do_c and r % 256 == 0:
                    w = min(256, tk - r)
                    d = _dot(vt_ref[vrows[i][0]:vrows[i][1], pl.ds(c_start + r, w)], p_ref[par, i, r:r + w, :])
                    pv = d if pv is None else pv + d
            if do_b:
                al_ref[1 - par, i] = jnp.exp2(m_prev - m_new)
                m_ref[i] = m_new
            if do_c:
                acc_ref[i] = al_ref[par, i] * acc_ref[i] + pv
            if do_a:
                cm_ref[par, i] = jnp.max(cmax, axis=0, keepdims=True)

    def finish_tile(tile):
        o0 = acc_ref[0, 0:vdim, :] / acc_ref[0, vdim:vdim + 1, :]
        o1 = acc_ref[1, 0:vdim, :] / acc_ref[1, vdim:vdim + 1, :]
        if mode == "da":
            lv = lam_ref[...]
            lam = (jnp.exp(jnp.sum(lv[0:1] * lv[1:2], axis=-1, keepdims=True))
                   - jnp.exp(jnp.sum(lv[2:3] * lv[3:4], axis=-1, keepdims=True)) + lam_init)
            o = o0 - lam * o1
            o = _rms(o, 0, SUBLN_EPS) * g_ref[...] * (1.0 - lam_init)
        else:
            o = jnp.concatenate([o0, o1], axis=0)
        row = tile * tq if isinstance(tile, int) else pl.multiple_of(tile * tq, tq)
        o_ref[pl.ds(row, tq), :] = o.T.astype(o_ref.dtype)

    def static_stage(f):
        stage(f, f % 2)
        if f >= 2 and (f - 2) % n_c == n_c - 1:
            finish_tile((f - 2) // n_c)

    for f in range(0, min(2, total + 2)):
        static_stage(f)

    def body(j, carry):
        f0 = 2 * j
        stage(f0, 0)
        stage(f0 + 1, 1)

        if n_c == 2:
            finish_tile(j - 1)
        else:
            @pl.when(lax.rem(f0 - 1, jnp.int32(n_c)) == n_c - 1)
            def _():
                finish_tile(lax.div(f0 - 1, jnp.int32(n_c)))
        return carry

    if total > 2:
        lax.fori_loop(1, total // 2, body, 0)
    for f in range(max(2, total), total + 2):
        static_stage(f)


def _dense_attention(mode, qt, k, vt, *, n_q, tq, tk, q_blk0, k_blk0, n_k, extra=(), lam_init=0.0):
    b = qt.shape[0]
    groups = N_HEADS if mode == "da" else N_HEADS // 2
    qw = 128 if mode == "da" else 256
    vblk = DA_VROWS if mode == "da" else 2 * MLA_VROWS
    acc_rows = DA_VROWS if mode == "da" else MLA_VROWS
    in_specs = [pl.BlockSpec((None, qw, n_q), lambda i, g: (i, g, q_blk0)),
                pl.BlockSpec((None, n_k, qw), lambda i, g: (i, k_blk0, g)),
                pl.BlockSpec((None, vblk, n_k), lambda i, g: (i, g, k_blk0))]
    args = [qt, k, vt]
    for a in extra:
        in_specs.append(pl.BlockSpec(a.shape, lambda i, g: (0, 0)))
        args.append(a)
    return pl.pallas_call(
        functools.partial(_attn_kernel, mode=mode, tq=tq, n_k=n_k, tk=tk, lam_init=lam_init),
        grid=(b, groups),
        in_specs=in_specs,
        out_specs=pl.BlockSpec((None, n_q, 128), lambda i, g: (i, 0, g)),
        out_shape=jax.ShapeDtypeStruct((b, n_q, 128 * groups), BF16),
        scratch_shapes=[pltpu.VMEM((2, 1, tq), F32),
                        pltpu.VMEM((2, acc_rows, tq), F32), pltpu.VMEM((2, 2, tk, tq), F32),
                        pltpu.VMEM((2, 2, 1, tq), F32), pltpu.VMEM((2, 2, tk, tq), BF16),
                        pltpu.VMEM((2, 2, 1, tq), F32)],
        compiler_params=_cparams(("parallel", "parallel")),
        name="attn_" + mode,
    )(*args)


def _na_kernel(*refs, rows, ctx_off, n_ctx, use_window):
    if use_window:
        qt_ref, k_ref, vt_ref, bias_ref, o_ref, s_ref, p_ref = refs
    else:
        qt_ref, k_ref, vt_ref, o_ref, s_ref, p_ref = refs
    qt = qt_ref[...]
    row = lax.broadcasted_iota(jnp.int32, qt.shape, 0)
    zero = jnp.zeros_like(qt)
    n_win = NA_WIN_ROWS * GRID_W if use_window else 0
    if use_window:
        i = pl.program_id(1)
        kr0 = jnp.clip(i * NA_Q_ROWS - NA_KH // 2, 0, rows - NA_WIN_ROWS)
        start = pl.multiple_of(kr0 * GRID_W, 128)
    maxima, outs = [], []
    for h in range(N_HEADS):
        qh = jnp.where((row >= h * NA_DIM) & (row < (h + 1) * NA_DIM), qt, zero)
        s_c = _dot(k_ref[ctx_off:ctx_off + n_ctx, :], qh)
        s_ref[h, n_win:n_win + n_ctx, :] = s_c
        cmax = _fold_rows(jnp.maximum, s_c, 8)
        if use_window:
            s_w = _dot(k_ref[pl.ds(start, n_win), :], qh) + bias_ref[h]
            s_ref[h, 0:n_win, :] = s_w
            cmax = jnp.maximum(cmax, _fold_rows(jnp.maximum, s_w, 8))
        maxima.append(jnp.max(cmax, axis=0, keepdims=True))
    for h in range(N_HEADS):
        for r in range(0, n_win + n_ctx, ROW_BLOCK):
            p_ref[h, r:r + ROW_BLOCK, :] = jnp.exp2((s_ref[h, r:r + ROW_BLOCK, :] - maxima[h]).astype(BF16))
    for h in range(N_HEADS):
        vrows = slice(h * NA_VROWS, (h + 1) * NA_VROWS)
        o = _dot(vt_ref[vrows, ctx_off:ctx_off + n_ctx], p_ref[h, n_win:n_win + n_ctx, :])
        if use_window:
            o = o + _dot(vt_ref[vrows, pl.ds(start, n_win)], p_ref[h, 0:n_win, :])
        outs.append(o[0:NA_DIM] / o[NA_DIM:NA_DIM + 1])
    o_ref[...] = jnp.concatenate(outs, axis=0).T.astype(o_ref.dtype)


def _na_bias_index(rows):
    nb = rows // NA_Q_ROWS
    assert rows % NA_Q_ROWS == 0 and nb >= 3 and rows >= NA_WIN_ROWS and (rows - NA_WIN_ROWS) % 2 == 0

    def block(i):
        r = i * NA_Q_ROWS + np.arange(NA_Q_ROWS)
        kr = int(np.clip(i * NA_Q_ROWS - NA_KH // 2, 0, rows - NA_WIN_ROWS)) + np.arange(NA_WIN_ROWS)
        rs = np.clip(r - NA_KH // 2, 0, rows - NA_KH)
        valid = (kr[None, :] >= rs[:, None]) & (kr[None, :] < rs[:, None] + NA_KH)
        dy = np.clip(kr[None, :] - r[:, None] + NA_KH - 1, 0, 2 * NA_KH - 2)
        return valid, dy

    blocks = [block(i) for i in range(nb)]
    for i in range(2, nb - 1):
        assert all(np.array_equal(a, b_) for a, b_ in zip(blocks[1], blocks[i]))
    cls = [blocks[0], blocks[1], blocks[nb - 1]]
    rvalid = np.stack([c[0] for c in cls]).reshape(-1)
    dy = np.stack([c[1] for c in cls]).reshape(-1)
    rsel = (dy[:, None] == np.arange(2 * NA_KH - 1)[None, :]).astype(np.float32)

    c = np.arange(GRID_W)
    cs = np.clip(c - NA_KW // 2, 0, GRID_W - NA_KW)
    cvalid = ((c[None, :] >= cs[:, None]) & (c[None, :] < cs[:, None] + NA_KW)).reshape(-1)
    dx = np.clip(c[None, :] - c[:, None] + NA_KW - 1, 0, 2 * NA_KW - 2).reshape(-1)
    csel = (np.arange(2 * NA_KW - 1)[:, None] == dx[None, :]).astype(np.float32)
    return rsel, rvalid.astype(np.float32)[:, None], csel, cvalid.astype(np.float32)[None, :]


def _na_bias_kernel(rpb_ref, rsel_ref, rmask_ref, csel_ref, cmask_ref, o_ref):
    cols = _dot_exact(rpb_ref[...], csel_ref[...])
    vals = _dot_exact(rsel_ref[...], cols)
    valid = (rmask_ref[...] * cmask_ref[...]) > 0.5
    o_ref[...] = jnp.where(valid, vals * LOG2E, NEG_BIG)


def _na_bias(rpb, rows):
    rsel, rmask, csel, cmask = (jnp.asarray(a) for a in _na_bias_index(rows))
    nh = rpb.shape[0]
    n_r, n_c = rsel.shape[0], csel.shape[1]
    full = lambda a: pl.BlockSpec(a.shape, lambda h: (0, 0))
    out = pl.pallas_call(
        _na_bias_kernel,
        grid=(nh,),
        in_specs=[pl.BlockSpec((None,) + rpb.shape[1:], lambda h: (h, 0, 0)),
                  full(rsel), full(rmask), full(csel), full(cmask)],
        out_specs=pl.BlockSpec((None, n_r, n_c), lambda h: (h, 0, 0)),
        out_shape=jax.ShapeDtypeStruct((nh, n_r, n_c), F32),
        compiler_params=_cparams(("parallel",)),
        name="na_bias",
    )(rpb.astype(F32), rsel, rmask, csel, cmask)
    out = out.reshape(nh, 3, NA_Q_ROWS, NA_WIN_ROWS, GRID_W, GRID_W)
    out = jnp.transpose(out, (1, 0, 3, 5, 2, 4))
    return out.reshape(3, nh, NA_WIN_ROWS * GRID_W, NA_Q_ROWS * GRID_W)


def _neighbourhood_attention(nqt, nk, nvt, bias, seq):
    b, t, w = nk.shape
    tq = NA_Q_ROWS * GRID_W
    nb = seq // tq
    rows = seq // GRID_W
    n_keys = NA_WIN_ROWS * GRID_W + t - seq
    return pl.pallas_call(
        functools.partial(_na_kernel, rows=rows, ctx_off=seq, n_ctx=t - seq, use_window=True),
        grid=(b, nb),
        in_specs=[pl.BlockSpec((None, w, tq), lambda i, j: (i, 0, j)),
                  pl.BlockSpec((None, t, w), lambda i, j: (i, 0, 0)),
                  pl.BlockSpec((None, nvt.shape[1], t), lambda i, j: (i, 0, 0)),
                  pl.BlockSpec((None,) + bias.shape[1:],
                               lambda i, j: (jnp.where(j == 0, 0, jnp.where(j == nb - 1, 2, 1)), 0, 0, 0))],
        out_specs=pl.BlockSpec((None, tq, w), lambda i, j: (i, j, 0)),
        out_shape=jax.ShapeDtypeStruct((b, seq, w), BF16),
        scratch_shapes=[pltpu.VMEM((N_HEADS, n_keys, tq), F32), pltpu.VMEM((N_HEADS, n_keys, tq), BF16)],
        compiler_params=_cparams(("parallel", "parallel")),
        name="attn_na",
    )(nqt, nk, nvt, bias)


def _neighbourhood_attention_ctx(nqt, nk, nvt, seq):
    b, t, w = nk.shape
    n_ctx = t - seq
    blk = seq // n_ctx
    return pl.pallas_call(
        functools.partial(_na_kernel, rows=0, ctx_off=0, n_ctx=n_ctx, use_window=False),
        grid=(b,),
        in_specs=[pl.BlockSpec((None, w, n_ctx), lambda i: (i, 0, blk)),
                  pl.BlockSpec((None, n_ctx, w), lambda i: (i, blk, 0)),
                  pl.BlockSpec((None, nvt.shape[1], n_ctx), lambda i: (i, 0, blk))],
        out_specs=pl.BlockSpec((None, n_ctx, w), lambda i: (i, 0, 0)),
        out_shape=jax.ShapeDtypeStruct((b, n_ctx, w), BF16),
        scratch_shapes=[pltpu.VMEM((N_HEADS, n_ctx, n_ctx), F32), pltpu.VMEM((N_HEADS, n_ctx, n_ctx), BF16)],
        compiler_params=_cparams(("parallel",)),
        name="attn_na_ctx",
    )(nqt, nk, nvt)


def _layer_norm(z, g, b):
    mu = jnp.mean(z, axis=-1, keepdims=True)
    zc = z - mu
    var = jnp.mean(jnp.square(zc), axis=-1, keepdims=True)
    return zc * lax.rsqrt(var + LN_EPS) * g + b


def _ffn_kernel(x_ref, ctx_ref, ya_ref, yb_ref, yc_ref, yac_ref, ybc_ref, ycc_ref, mod_ref, wo_ref, w1_ref,
                w2_ref, ln_ref, o_ref, *, alpha, ctx_tile):
    is_ctx = pl.program_id(1) == ctx_tile
    x = jnp.where(is_ctx, ctx_ref[...], x_ref[...])
    ya, yb, yc = (jnp.where(is_ctx, c[...], l[...])
                  for l, c in ((ya_ref, yac_ref), (yb_ref, ybc_ref), (yc_ref, ycc_ref)))
    a = _dot(ya, wo_ref[0:512, :]) + _dot(yb, wo_ref[512:768, :]) + _dot(yc, wo_ref[768:1024, :])
    x1 = _layer_norm(alpha * x + mod_ref[2:3, :] * a, ln_ref[0:1, :], ln_ref[1:2, :])
    h2 = (x1 * (1.0 + mod_ref[4:5, :]) + mod_ref[3:4, :]).astype(BF16)
    u = jnp.maximum(_dot(h2, w1_ref[...]), 0.0)
    f = _dot(jnp.square(u).astype(BF16), w2_ref[...])
    o_ref[...] = _layer_norm(alpha * x1 + mod_ref[5:6, :] * f, ln_ref[2:3, :], ln_ref[3:4, :])


def _out_ffn(x_src, ctx_src, ctx_blk, seq, y_lat, y_ctx, mod_l, layer, wo, w1, w2, ln, n_tok, n_ctx_row, alpha):
    b, _, d = x_src.shape
    tm = TOK_TILE
    ctx_tile = seq // tm
    nt = n_tok // tm

    def tok(width):
        return pl.BlockSpec((None, tm, width), lambda i, j: (i, j, 0))

    def lat(width):
        return pl.BlockSpec((None, tm, width), lambda i, j: (i, jnp.minimum(j, ctx_tile - 1), 0))

    def ctx_side(width, blk=0):
        return pl.BlockSpec((None, tm, width), lambda i, j: (i, blk, 0))

    def const(arr):
        return pl.BlockSpec(arr.shape, lambda i, j: (0,) * arr.ndim, pipeline_mode=pl.Buffered(1))

    def layer_of(arr):
        return pl.BlockSpec((None,) + arr.shape[1:], lambda i, j: (layer,) + (0,) * (arr.ndim - 1),
                            pipeline_mode=pl.Buffered(1))

    return pl.pallas_call(
        functools.partial(_ffn_kernel, alpha=alpha, ctx_tile=ctx_tile),
        grid=(b, nt),
        in_specs=[lat(d), ctx_side(d, ctx_blk), lat(512), lat(256), lat(256),
                  ctx_side(512), ctx_side(256), ctx_side(256),
                  pl.BlockSpec((None, 6, d), lambda i, j: (jnp.where(j == ctx_tile, n_ctx_row, i), 0, 0)),
                  layer_of(wo), layer_of(w1), layer_of(w2), const(ln)],
        out_specs=tok(d),
        out_shape=jax.ShapeDtypeStruct((b, n_tok, d), F32),
        compiler_params=_cparams(("parallel", "parallel")),
        name="out_ffn",
    )(x_src, ctx_src, *y_lat, *y_ctx, mod_l, wo, w1, w2, ln)


def _rope_tables(seq, n_ctx):
    t = jnp.arange(seq, dtype=jnp.int32)
    row = (t // GRID_W).astype(F32)
    col = (t % GRID_W).astype(F32)

    def axial(n):
        half = n // 2
        inv = ROPE_BASE ** (-2.0 * jnp.arange(half, dtype=F32) / n)
        ar, ac = row[:, None] * inv[None, :], col[:, None] * inv[None, :]
        cos = jnp.concatenate([jnp.cos(ar), jnp.cos(ar), jnp.cos(ac), jnp.cos(ac)], axis=-1)
        sin = jnp.concatenate([-jnp.sin(ar), jnp.sin(ar), -jnp.sin(ac), jnp.sin(ac)], axis=-1)
        cos = jnp.concatenate([cos, jnp.ones((n_ctx, 2 * n), F32)], axis=0)
        sin = jnp.concatenate([sin, jnp.zeros((n_ctx, 2 * n), F32)], axis=0)
        return cos, sin

    cos_a, sin_a = axial(DA_DIM // 2)
    cos_m, sin_m = axial(MLA_ROPE // 2)
    sa = DA_DIM ** -0.5 * LOG2E
    sm = (MLA_NOPE + MLA_ROPE) ** -0.5 * LOG2E
    tt = seq + n_ctx
    ones = jnp.ones((tt, MLA_NOPE), F32)
    zpad = jnp.zeros((tt, MLA_ZPAD), F32)
    return dict(
        caq=cos_a.T * sa, saq=sin_a.T * sa,
        cak=jnp.tile(cos_a, (1, 2)), sak=jnp.tile(sin_a, (1, 2)),
        cmq=cos_m.T * sm, smq=sin_m.T * sm,
        cmk=jnp.concatenate([ones, cos_m, zpad], axis=-1),
        smk=jnp.concatenate([0.0 * ones, sin_m, zpad], axis=-1))


def _layer_weights(w_in, gq, w_uq, gkv, w_ukv):
    d = w_in.shape[0]
    o = np.cumsum([0, 512, 512, 512, MLA_Q_RANK, MLA_KV_RANK, MLA_ROPE, 256, 256, 256])
    col = lambda k: w_in[:, int(o[k]):int(o[k + 1])]
    aq, ak, av, cq, ckv, kr, nq, nk, nv = (col(k) for k in range(9))
    kr_placed = jnp.concatenate([jnp.zeros((d, MLA_NOPE), w_in.dtype), kr,
                                 jnp.zeros((d, MLA_ZPAD), w_in.dtype)], axis=1)
    wtok = jnp.concatenate([ak] + [kr_placed] * N_HEADS + [ckv, nk], axis=1).astype(BF16)
    wt = jnp.concatenate([aq, av, cq, ckv, nq, nv], axis=1).T.astype(BF16)
    uq = w_uq.reshape(MLA_Q_RANK, N_HEADS, MLA_NOPE + MLA_ROPE)
    uq = jnp.pad(uq, ((0, 0), (0, 0), (0, MLA_ZPAD)))
    ukv = w_ukv.reshape(MLA_KV_RANK, N_HEADS, MLA_NOPE + MLA_VDIM)
    uk = jnp.pad(ukv[:, :, :MLA_NOPE], ((0, 0), (0, 0), (0, HEAD_PAD - MLA_NOPE)))
    return dict(
        wtok=wtok, wt=wt,
        gqt=gq.reshape(-1, 1).astype(F32),
        wuqt=uq.reshape(MLA_Q_RANK, N_HEADS * HEAD_PAD).T.astype(BF16),
        gkv=gkv.reshape(1, -1).astype(F32), gkvt=gkv.reshape(-1, 1).astype(F32),
        wuk=uk.reshape(MLA_KV_RANK, N_HEADS * HEAD_PAD).astype(BF16),
        wuvt=ukv[:, :, MLA_NOPE:].reshape(MLA_KV_RANK, N_HEADS * MLA_VDIM).T.astype(BF16))


def _pick_key_chunk(n_k, tk_max):
    for n_c in range(2, n_k // FINE + 1, 2):
        if n_k % (n_c * FINE) == 0 and n_k // n_c <= tk_max:
            return n_k // n_c
    raise ValueError(f"no even chunking of {n_k} keys")


def kernel(x, c, ctx, c_ctx, w_mod, b_mod, w_in, da_lam_q1, da_lam_k1, da_lam_q2, da_lam_k2, da_subln_g,
           mla_q_norm_g, mla_w_uq, mla_kv_norm_g, mla_w_ukv, na_rpb, w_out, ln1_g, ln1_b, w_ff1, w_ff2,
           ln2_g, ln2_b):
    b, seq, d = x.shape
    n_ctx = ctx.shape[1]
    depth = w_mod.shape[0]
    assert seq % TOK_TILE == 0 and n_ctx == TOK_TILE and b < 8 and seq % (NA_Q_ROWS * GRID_W) == 0
    alpha = (2.0 * depth) ** 0.25
    tq = ATTN_TQ
    tk_da, tk_mla = (_pick_key_chunk(seq + n_ctx, m) for m in (DA_TK_MAX, MLA_TK_MAX))
    assert seq % tq == 0 and n_ctx % FINE == 0

    cs = jnp.zeros((8, d), F32).at[:b].set(c).at[b].set(c_ctx)
    mod = _modulation(cs, w_mod, b_mod).reshape(depth, 8, 6, d)
    tabs = _rope_tables(seq, n_ctx)
    ctx_blk = seq // n_ctx
    x_src, ctx_src, src_blk = x, ctx, 0
    wo_all, w1_all, w2_all = (w.astype(BF16) for w in (w_out, w_ff1, w_ff2))

    for l in range(depth):
        last = l == depth - 1
        lam_init = 0.8 - 0.6 * math.exp(-0.3 * l)
        pw = _layer_weights(w_in[l], mla_q_norm_g[l], mla_w_uq[l], mla_kv_norm_g[l], mla_w_ukv[l])
        qat, ka, vat, qmt, km, vmt, nqt, nk, nvt = _projection(x_src, ctx_src, src_blk, seq, mod[l], pw,
                                                                tabs, b)

        lam_vec = jnp.stack([da_lam_q1[l], da_lam_k1[l], da_lam_q2[l], da_lam_k2[l]]).astype(F32)
        g_sub = da_subln_g[l].reshape(-1, 1).astype(F32)
        dense = dict(n_q=seq, tq=tq, q_blk0=0, k_blk0=0, n_k=seq + n_ctx)
        y_lat = (_dense_attention("da", qat, ka, vat, tk=tk_da, extra=(lam_vec, g_sub), lam_init=lam_init, **dense),
                 _dense_attention("mla", qmt, km, vmt, tk=tk_mla, **dense),
                 _neighbourhood_attention(nqt, nk, nvt, _na_bias(na_rpb[l], seq // GRID_W), seq))
        y_ctx = y_lat
        if not last:
            dense = dict(n_q=n_ctx, tq=n_ctx, tk=n_ctx, q_blk0=ctx_blk, k_blk0=ctx_blk, n_k=n_ctx)
            y_ctx = (_dense_attention("da", qat, ka, vat, extra=(lam_vec, g_sub), lam_init=lam_init, **dense),
                     _dense_attention("mla", qmt, km, vmt, **dense),
                     _neighbourhood_attention_ctx(nqt, nk, nvt, seq))

        ln = jnp.stack([ln1_g[l], ln1_b[l], ln2_g[l], ln2_b[l]]).astype(F32)
        x_src = _out_ffn(x_src, ctx_src, src_blk, seq, y_lat, y_ctx, mod[l], l, wo_all, w1_all, w2_all, ln,
                         seq if last else seq + n_ctx, b, alpha)
        ctx_src, src_blk = x_src, ctx_blk
    return x_src
```
